```python
import math
import jax, jax.numpy as jnp
from jax import lax
import numpy as np

D_MODEL = 4096
BATCH = 2
SEQ = 4096
DEPTH = 4

MIX_WIDTH = D_MODEL
SSM_WIDTH = MIX_WIDTH // 2
ATTN_WIDTH = MIX_WIDTH - SSM_WIDTH
SSM_GROUP = 16
SSM_GROUPS = SSM_WIDTH // SSM_GROUP
SSM_STATE = 64
HEAD_DIM = 64
N_Q_HEADS = ATTN_WIDTH // HEAD_DIM
N_KV_HEADS = max(N_Q_HEADS // 8, 1)
GQA_GROUP = N_Q_HEADS // N_KV_HEADS
KV_WIDTH = N_KV_HEADS * HEAD_DIM
IN_WIDTH = SSM_WIDTH + ATTN_WIDTH + 2 * KV_WIDTH
WINDOW = 128
PEER_HEADS = 8
PEER_KEYS = 128
PEER_EXPERTS = PEER_KEYS * PEER_KEYS
PEER_QDIM = 128
PEER_HALF = PEER_QDIM // 2
PEER_TOPK = 16
PEER_CHUNK = 128
N_MOD = 6
EPS = 1e-6

kernel_name = "hybrid_s5_swa_peer_adaln_trunk"


def rmsnorm(x, g):
    xf = x.astype(jnp.float32)
    y = xf * lax.rsqrt(jnp.mean(xf * xf, axis=-1, keepdims=True) + EPS)
    return (y * g.astype(jnp.float32)).astype(x.dtype)


def alibi_slopes(n):
    return jnp.exp2(-8.0 * jnp.arange(1, n + 1, dtype=jnp.float32) / n)


def _ssm_combine(e1, e2):
    a1r, a1i, b1r, b1i = e1
    a2r, a2i, b2r, b2i = e2
    ar = a2r * a1r - a2i * a1i
    ai = a2r * a1i + a2i * a1r
    br = a2r * b1r - a2i * b1i + b2r
    bi = a2r * b1i + a2i * b1r + b2i
    return (ar, ai, br, bi)


def s5_mixer(u, lam_re, lam_im, log_dt, b_re, b_im, c_re, c_im, d_skip, w_glu):
    bsz, seq, _ = u.shape
    f32 = jnp.float32
    ug = u.astype(f32).reshape(bsz, seq, SSM_GROUPS, SSM_GROUP)
    dt = jnp.exp(log_dt.astype(f32))[:, None]
    lr = lam_re.astype(f32)
    li = lam_im.astype(f32)
    mag = jnp.exp(lr * dt)
    ar = mag * jnp.cos(li * dt)
    ai = mag * jnp.sin(li * dt)
    den = lr * lr + li * li
    kr = ((ar - 1.0) * lr + ai * li) / den
    ki = (ai * lr - (ar - 1.0) * li) / den
    br = b_re.astype(f32)
    bi = b_im.astype(f32)
    bbr = kr[..., None] * br - ki[..., None] * bi
    bbi = kr[..., None] * bi + ki[..., None] * br
    xr = jnp.einsum('blgh,gph->blgp', ug, bbr)
    xi = jnp.einsum('blgh,gph->blgp', ug, bbi)
    arb = jnp.broadcast_to(ar, xr.shape)
    aib = jnp.broadcast_to(ai, xr.shape)
    _, _, sr, si = lax.associative_scan(_ssm_combine, (arb, aib, xr, xi), axis=1)
    y = (jnp.einsum('blgp,ghp->blgh', sr, c_re.astype(f32))
         - jnp.einsum('blgp,ghp->blgh', si, c_im.astype(f32))
         + d_skip.astype(f32) * ug)
    g = jax.nn.gelu(y.reshape(bsz, seq, SSM_WIDTH)).astype(u.dtype)
    return g * jax.nn.sigmoid(g @ w_glu)


def swa_attention(q, k, v, q_gain, k_gain, sinks):
    bsz, seq, _ = q.shape
    nb = seq // WINDOW
    q = rmsnorm(q.reshape(bsz, seq, N_Q_HEADS, HEAD_DIM), q_gain)
    k = rmsnorm(k.reshape(bsz, seq, N_KV_HEADS, HEAD_DIM), k_gain)
    v = v.reshape(bsz, seq, N_KV_HEADS, HEAD_DIM)
    qb = q.reshape(bsz, nb, WINDOW, N_KV_HEADS, GQA_GROUP, HEAD_DIM)

    def band(t):
        tp = jnp.concatenate([jnp.zeros_like(t[:, :WINDOW]), t], axis=1)
        tp = tp.reshape(bsz, nb + 1, WINDOW, N_KV_HEADS, HEAD_DIM)
        return jnp.concatenate([tp[:, :-1], tp[:, 1:]], axis=2)

    kb = band(k)
    vb = band(v)
    scores = jnp.einsum('bnqhgd,bnkhd->bnhgqk', qb, kb,
                        preferred_element_type=jnp.float32) * (HEAD_DIM ** -0.5)
    t_loc = jnp.arange(WINDOW)[:, None]
    s_loc = jnp.arange(2 * WINDOW)[None, :]
    dist = t_loc + WINDOW - s_loc
    in_win = (dist >= 0) & (dist < WINDOW)
    key_pos = jnp.arange(nb)[:, None, None] * WINDOW + s_loc[None] - WINDOW
    valid = in_win[None] & (key_pos >= 0)
    slopes = alibi_slopes(N_Q_HEADS).reshape(N_KV_HEADS, GQA_GROUP, 1, 1)
    bias = -slopes * dist.astype(jnp.float32)
    scores = jnp.where(valid[None, :, None, None], scores + bias, -jnp.inf)
    sink = jnp.broadcast_to(
        sinks.astype(jnp.float32).reshape(1, 1, N_KV_HEADS, GQA_GROUP, 1, 1),
        scores.shape[:-1] + (1,))
    probs = jax.nn.softmax(jnp.concatenate([scores, sink], axis=-1), axis=-1)[..., :-1]
    out = jnp.einsum('bnhgqk,bnkhd->bnqhgd', probs.astype(v.dtype), vb)
    return out.reshape(bsz, seq, ATTN_WIDTH)


def peer_ffn(h, w_q, k1, k2, u_tab, v_tab):
    bsz, seq, dm = h.shape
    t = h.reshape(-1, dm)
    n_tok = t.shape[0]
    q = (t @ w_q).astype(jnp.float32).reshape(n_tok, PEER_HEADS, PEER_QDIM)
    s1 = jnp.einsum('thd,nd->thn', q[..., :PEER_HALF], k1.astype(jnp.float32))
    s2 = jnp.einsum('thd,nd->thn', q[..., PEER_HALF:], k2.astype(jnp.float32))
    v1, i1 = lax.top_k(s1, PEER_TOPK)
    v2, i2 = lax.top_k(s2, PEER_TOPK)
    cand = (v1[..., :, None] + v2[..., None, :]).reshape(n_tok, PEER_HEADS, PEER_TOPK * PEER_TOPK)
    vs, ic = lax.top_k(cand, PEER_TOPK)
    ia = jnp.take_along_axis(i1, ic // PEER_TOPK, axis=-1)
    ib = jnp.take_along_axis(i2, ic % PEER_TOPK, axis=-1)
    ids = (ia * PEER_KEYS + ib).reshape(n_tok, PEER_HEADS * PEER_TOPK)
    gates = jax.nn.softmax(vs, axis=-1).reshape(n_tok, PEER_HEADS * PEER_TOPK)
    n_chunks = n_tok // PEER_CHUNK

    def chunk_fn(args):
        tc, idc, gc = args
        act = jax.nn.gelu(jnp.einsum('cd,ckd->ck', tc, u_tab[idc]).astype(jnp.float32))
        w = (gc * act).astype(tc.dtype)
        return jnp.einsum('ck,ckd->cd', w, v_tab[idc])

    out = lax.map(chunk_fn, (t.reshape(n_chunks, PEER_CHUNK, dm),
                             ids.reshape(n_chunks, PEER_CHUNK, -1),
                             gates.reshape(n_chunks, PEER_CHUNK, -1)))
    return out.reshape(bsz, seq, dm)


def setup_inputs(seed: int = 0) -> dict:
    key = jax.random.key(seed)
    ks = jax.random.split(key, 28)
    f32 = jnp.float32

    def nrm(k, shape, scale):
        return jax.random.normal(k, shape, f32) * scale

    n_idx = jnp.arange(SSM_STATE, dtype=f32)
    return {
        "x": nrm(ks[0], (BATCH, SEQ, D_MODEL), 1.0),
        "c": nrm(ks[1], (BATCH, D_MODEL), 1.0),
        "w_ada": nrm(ks[2], (D_MODEL, N_MOD * D_MODEL), 0.5 * D_MODEL ** -0.5),
        "b_ada": nrm(ks[3], (N_MOD * D_MODEL,), 0.01),
        "ada_layer": nrm(ks[4], (DEPTH, N_MOD, D_MODEL), 0.02),
        "norm1_g": 1.0 + nrm(ks[5], (DEPTH, D_MODEL), 0.02),
        "norm2_g": 1.0 + nrm(ks[6], (DEPTH, D_MODEL), 0.02),
        "w_in": nrm(ks[7], (DEPTH, D_MODEL, IN_WIDTH), D_MODEL ** -0.5),
        "lam_re": -0.5 + nrm(ks[8], (DEPTH, SSM_GROUPS, SSM_STATE), 0.01),
        "lam_im": jnp.pi * n_idx + nrm(ks[9], (DEPTH, SSM_GROUPS, SSM_STATE), 0.01),
        "log_dt": jax.random.uniform(ks[10], (DEPTH, SSM_GROUPS), f32, math.log(1e-3), math.log(1e-1)),
        "b_re": nrm(ks[11], (DEPTH, SSM_GROUPS, SSM_STATE, SSM_GROUP), (2 * SSM_GROUP) ** -0.5),
        "b_im": nrm(ks[12], (DEPTH, SSM_GROUPS, SSM_STATE, SSM_GROUP), (2 * SSM_GROUP) ** -0.5),
        "c_re": nrm(ks[13], (DEPTH, SSM_GROUPS, SSM_GROUP, SSM_STATE), (2 * SSM_STATE) ** -0.5),
        "c_im": nrm(ks[14], (DEPTH, SSM_GROUPS, SSM_GROUP, SSM_STATE), (2 * SSM_STATE) ** -0.5),
        "d_skip": 1.0 + nrm(ks[15], (DEPTH, SSM_GROUPS, SSM_GROUP), 0.1),
        "w_glu": nrm(ks[16], (DEPTH, SSM_WIDTH, SSM_WIDTH), SSM_WIDTH ** -0.5),
        "q_gain": 1.0 + nrm(ks[17], (DEPTH, HEAD_DIM), 0.02),
        "k_gain": 1.0 + nrm(ks[18], (DEPTH, HEAD_DIM), 0.02),
        "sinks": nrm(ks[19], (DEPTH, N_Q_HEADS), 0.5),
        "gn_ssm": 1.0 + nrm(ks[20], (DEPTH, SSM_WIDTH), 0.02),
        "gn_attn": 1.0 + nrm(ks[21], (DEPTH, ATTN_WIDTH), 0.02),
        "w_out": nrm(ks[22], (DEPTH, MIX_WIDTH, D_MODEL), MIX_WIDTH ** -0.5),
        "peer_wq": nrm(ks[23], (DEPTH, D_MODEL, PEER_HEADS * PEER_QDIM), D_MODEL ** -0.5),
        "peer_k1": nrm(ks[24], (DEPTH, PEER_KEYS, PEER_HALF), PEER_HALF ** -0.5),
        "peer_k2": nrm(ks[25], (DEPTH, PEER_KEYS, PEER_HALF), PEER_HALF ** -0.5),
        "peer_u": nrm(ks[26], (DEPTH, PEER_EXPERTS, D_MODEL), D_MODEL ** -0.5),
        "peer_v": nrm(ks[27], (DEPTH, PEER_EXPERTS, D_MODEL), 0.5),
    }


def reference(x, c, w_ada, b_ada, ada_layer, norm1_g, norm2_g, w_in, lam_re, lam_im, log_dt,
              b_re, b_im, c_re, c_im, d_skip, w_glu, q_gain, k_gain, sinks, gn_ssm, gn_attn,
              w_out, peer_wq, peer_k1, peer_k2, peer_u, peer_v):
    dtype = x.dtype
    f32 = jnp.float32
    cond = (jax.nn.silu(c.astype(f32)) @ w_ada.astype(f32) + b_ada.astype(f32)).reshape(
        c.shape[0], N_MOD, D_MODEL)
    q_end = SSM_WIDTH + ATTN_WIDTH
    k_end = q_end + KV_WIDTH
    for l in range(DEPTH):
        mod = (cond + ada_layer[l].astype(f32))[:, :, None, :].astype(dtype)
        shift1, scale1, gate1, shift2, scale2, gate2 = (mod[:, i] for i in range(N_MOD))
        h = rmsnorm(x, norm1_g[l]) * (1 + scale1) + shift1
        z = h @ w_in[l]
        y_ssm = s5_mixer(z[..., :SSM_WIDTH], lam_re[l], lam_im[l], log_dt[l], b_re[l], b_im[l],
                         c_re[l], c_im[l], d_skip[l], w_glu[l])
        y_attn = swa_attention(z[..., SSM_WIDTH:q_end], z[..., q_end:k_end], z[..., k_end:],
                               q_gain[l], k_gain[l], sinks[l])
        mixed = jnp.concatenate([rmsnorm(y_ssm, gn_ssm[l]), rmsnorm(y_attn, gn_attn[l])],
                                axis=-1) @ w_out[l]
        x = x + gate1 * mixed
        h2 = rmsnorm(x, norm2_g[l]) * (1 + scale2) + shift2
        x = x + gate2 * peer_ffn(h2, peer_wq[l], peer_k1[l], peer_k2[l], peer_u[l], peer_v[l])
    return x
```

```python
import functools
import math

import jax
import jax.numpy as jnp
from jax import lax
from jax.experimental import pallas as pl
from jax.experimental.pallas import tpu as pltpu

F32 = jnp.float32
BF16 = jnp.bfloat16
EPS = 1e-6
WINDOW = 128
PEER_TOPK = 16
S5_CHUNK = 16
S5_GROUP_BLOCK = 8
V7X_VMEM_LIMIT = 56 * 1024 * 1024


def _params(semantics, vmem=V7X_VMEM_LIMIT):
    return pltpu.CompilerParams(dimension_semantics=semantics, vmem_limit_bytes=vmem)


def _mm_kernel(a_ref, b_ref, o_ref):
    o_ref[...] = jnp.dot(a_ref[...], b_ref[...], preferred_element_type=F32).astype(o_ref.dtype)


def _matmul(a, b, out_dtype, tm, tn):
    m, k = a.shape
    n = b.shape[1]
    return pl.pallas_call(
        _mm_kernel,
        grid=(m // tm, n // tn),
        in_specs=[pl.BlockSpec((tm, k), lambda i, j: (i, 0)),
                  pl.BlockSpec((k, tn), lambda i, j: (0, j))],
        out_specs=pl.BlockSpec((tm, tn), lambda i, j: (i, j)),
        out_shape=jax.ShapeDtypeStruct((m, n), out_dtype),
        compiler_params=_params(("parallel", "parallel")),
        name="matmul",
    )(a, b)


def _mm2_kernel(a1_ref, a2_ref, b1_ref, b2_ref, o_ref):
    acc = jnp.dot(a1_ref[...], b1_ref[...], preferred_element_type=F32)
    acc += jnp.dot(a2_ref[...], b2_ref[...], preferred_element_type=F32)
    o_ref[...] = acc.astype(o_ref.dtype)


def _matmul_concat(a1, a2, b, out_dtype, tm, tn):
    m, k1 = a1.shape
    k2 = a2.shape[1]
    assert k1 == k2
    n = b.shape[1]
    return pl.pallas_call(
        _mm2_kernel,
        grid=(m // tm, n // tn),
        in_specs=[pl.BlockSpec((tm, k1), lambda i, j: (i, 0)),
                  pl.BlockSpec((tm, k2), lambda i, j: (i, 0)),
                  pl.BlockSpec((k1, tn), lambda i, j: (0, j)),
                  pl.BlockSpec((k2, tn), lambda i, j: (1, j))],
        out_specs=pl.BlockSpec((tm, tn), lambda i, j: (i, j)),
        out_shape=jax.ShapeDtypeStruct((m, n), out_dtype),
        compiler_params=_params(("parallel", "parallel")),
        name="matmul_concat",
    )(a1, a2, b, b)


def _mm_nt_kernel(a_ref, b_ref, o_ref):
    o_ref[...] = lax.dot_general(a_ref[...], b_ref[...], (((1,), (1,)), ((), ())),
                                 preferred_element_type=F32).astype(o_ref.dtype)


def _matmul_nt(a, b, out_dtype, tn):
    m, k = a.shape
    n = b.shape[0]
    return pl.pallas_call(
        _mm_nt_kernel,
        grid=(n // tn,),
        in_specs=[pl.BlockSpec((m, k), lambda j: (0, 0)),
                  pl.BlockSpec((tn, k), lambda j: (j, 0))],
        out_specs=pl.BlockSpec((m, tn), lambda j: (0, j)),
        out_shape=jax.ShapeDtypeStruct((m, n), out_dtype),
        compiler_params=_params(("parallel",)),
        name="matmul_nt",
    )(a, b)


def _cond_kernel(c_ref, w_ref, b_ref, o_ref):
    c = c_ref[...]
    s = c * jax.nn.sigmoid(c)
    o_ref[...] = jnp.dot(s, w_ref[...], preferred_element_type=F32,
                         precision=lax.Precision.HIGHEST) + b_ref[...]


def _cond(c, w_ada, b_ada, tn=512):
    bsz, d = c.shape
    n = w_ada.shape[1]
    rows = 8
    cp = jnp.zeros((rows, d), F32).at[:bsz].set(c.astype(F32))
    out = pl.pallas_call(
        _cond_kernel,
        grid=(n // tn,),
        in_specs=[pl.BlockSpec((rows, d), lambda j: (0, 0)),
                  pl.BlockSpec((d, tn), lambda j: (0, j)),
                  pl.BlockSpec((1, tn), lambda j: (0, j))],
        out_specs=pl.BlockSpec((rows, tn), lambda j: (0, j)),
        out_shape=jax.ShapeDtypeStruct((rows, n), F32),
        compiler_params=_params(("parallel",)),
        name="adaln_cond",
    )(cp, w_ada.astype(F32), b_ada.astype(F32).reshape(1, n))
    return out[:bsz]


def _norm_body(x, g_ref, scale_ref, shift_ref, h_ref):
    y = x * lax.rsqrt(jnp.mean(x * x, axis=-1, keepdims=True) + EPS)
    y = y * g_ref[...]
    h_ref[...] = (y * (1.0 + scale_ref[0]) + shift_ref[0]).astype(h_ref.dtype)


def _norm_kernel(x_ref, g_ref, scale_ref, shift_ref, h_ref):
    _norm_body(x_ref[...], g_ref, scale_ref, shift_ref, h_ref)


def _resnorm_kernel(x_ref, d_ref, gate_ref, g_ref, scale_ref, shift_ref, xo_ref, h_ref):
    x = x_ref[...] + gate_ref[0] * d_ref[...]
    xo_ref[...] = x
    _norm_body(x, g_ref, scale_ref, shift_ref, h_ref)


def _adaln_norm(x, delta, gate, g, scale, shift, rows_per_batch, tm=256):
    t, d = x.shape
    tpb = rows_per_batch // tm
    row = pl.BlockSpec((tm, d), lambda i: (i, 0))
    per_batch = pl.BlockSpec((1, 1, d), lambda i: (i // tpb, 0, 0))
    vec = pl.BlockSpec((1, d), lambda i: (0, 0))
    g2 = g.astype(F32).reshape(1, d)
    b3 = lambda v: v.astype(F32).reshape(v.shape[0], 1, d)
    if delta is None:
        h = pl.pallas_call(
            _norm_kernel,
            grid=(t // tm,),
            in_specs=[row, vec, per_batch, per_batch],
            out_specs=row,
            out_shape=jax.ShapeDtypeStruct((t, d), BF16),
            compiler_params=_params(("parallel",)),
            name="adaln_norm",
        )(x, g2, b3(scale), b3(shift))
        return x, h
    return pl.pallas_call(
        _resnorm_kernel,
        grid=(t // tm,),
        in_specs=[row, row, per_batch, vec, per_batch, per_batch],
        out_specs=[row, row],
        out_shape=[jax.ShapeDtypeStruct((t, d), F32), jax.ShapeDtypeStruct((t, d), BF16)],
        compiler_params=_params(("parallel",)),
        name="residual_adaln_norm",
    )(x, delta, b3(gate), g2, b3(scale), b3(shift))


def _residual_kernel(x_ref, d_ref, gate_ref, o_ref):
    o_ref[...] = x_ref[...] + gate_ref[0] * d_ref[...]


def _residual_add(x, delta, gate, rows_per_batch, tm=256):
    t, d = x.shape
    tpb = rows_per_batch // tm
    row = pl.BlockSpec((tm, d), lambda i: (i, 0))
    return pl.pallas_call(
        _residual_kernel,
        grid=(t // tm,),
        in_specs=[row, row, pl.BlockSpec((1, 1, d), lambda i: (i // tpb, 0, 0))],
        out_specs=row,
        out_shape=jax.ShapeDtypeStruct((t, d), F32),
        compiler_params=_params(("parallel",)),
        name="residual_add",
    )(x, delta, gate.astype(F32).reshape(gate.shape[0], 1, d))


def _s5_operators(lam_re, lam_im, log_dt, b_re, b_im, c_re, c_im, d_skip):
    hp = lax.Precision.HIGHEST
    j = S5_CHUNK
    f = lambda v: v.astype(F32)
    lr, li = f(lam_re), f(lam_im)
    g, p = lr.shape
    h = b_re.shape[-1]
    dt = jnp.exp(f(log_dt))[:, None]
    mag = jnp.exp(lr * dt)
    ar = mag * jnp.cos(li * dt)
    ai = mag * jnp.sin(li * dt)
    den = lr * lr + li * li
    kr = ((ar - 1.0) * lr + ai * li) / den
    ki = (ai * lr - (ar - 1.0) * li) / den
    br, bi = f(b_re), f(b_im)
    bbr = kr[..., None] * br - ki[..., None] * bi
    bbi = kr[..., None] * bi + ki[..., None] * br
    cr, ci = f(c_re), f(c_im)
    pr, pi = [jnp.ones_like(ar)], [jnp.zeros_like(ar)]
    for _ in range(j):
        pr.append(pr[-1] * ar - pi[-1] * ai)
        pi.append(pr[-2] * ai + pi[-1] * ar)
    pr, pi = jnp.stack(pr), jnp.stack(pi)
    er = pr[:j, :, :, None] * bbr - pi[:j, :, :, None] * bbi
    ei = pr[:j, :, :, None] * bbi + pi[:j, :, :, None] * bbr
    kk = (jnp.einsum('ghp,ngpk->nghk', cr, er, precision=hp)
          - jnp.einsum('ghp,ngpk->nghk', ci, ei, precision=hp))
    kk = kk.at[0].add(f(d_skip)[:, :, None] * jnp.eye(h, dtype=F32))
    lag = jnp.arange(j)[None, :] - jnp.arange(j)[:, None]
    tt = jnp.where((lag >= 0)[:, :, None, None, None], kk[jnp.clip(lag, 0)], 0.0)
    t_op = tt.transpose(2, 0, 4, 1, 3).reshape(g, j * h, j * h)
    bc_r = er[::-1].transpose(1, 0, 3, 2).reshape(g, j * h, p)
    bc_i = ei[::-1].transpose(1, 0, 3, 2).reshape(g, j * h, p)
    p1r, p1i = pr[1:], pi[1:]
    cc_r = cr[None] * p1r[:, :, None, :] - ci[None] * p1i[:, :, None, :]
    cc_i = -(cr[None] * p1i[:, :, None, :] + ci[None] * p1r[:, :, None, :])
    cc_r = cc_r.transpose(1, 3, 0, 2).reshape(g, p, j * h)
    cc_i = cc_i.transpose(1, 3, 0, 2).reshape(g, p, j * h)

    def pair_in(m):
        m = m.reshape(g // 2, 2, j * h, p)
        z = jnp.zeros_like(m[:, 0])
        return jnp.concatenate([jnp.concatenate([m[:, 0], z], -1),
                                jnp.concatenate([z, m[:, 1]], -1)], 1)

    def pair_out(m):
        m = m.reshape(g // 2, 2, p, j * h)
        z = jnp.zeros_like(m[:, 0])
        return jnp.concatenate([jnp.concatenate([m[:, 0], z], -1),
                                jnp.concatenate([z, m[:, 1]], -1)], 1)

    nblk = g // S5_GROUP_BLOCK
    a16r = pr[j].reshape(nblk, 1, S5_GROUP_BLOCK * p)
    a16i = pi[j].reshape(nblk, 1, S5_GROUP_BLOCK * p)
    return (t_op.astype(BF16), pair_in(bc_r).astype(BF16), pair_in(bc_i).astype(BF16),
            pair_out(cc_r).astype(BF16), pair_out(cc_i).astype(BF16), a16r, a16i)


def _s5_kernel(u_ref, t_ref, bre_ref, bim_ref, cre_ref, cim_ref, ar_ref, ai_ref, o_ref,
               zre, zim, pre, pim, *, chunks_per_batch, n_batch):
    npairs = S5_GROUP_BLOCK // 2
    lanes = zre.shape[1] // npairs
    for pp in range(npairs):
        up = jnp.concatenate([u_ref[2 * pp], u_ref[2 * pp + 1]], axis=1)
        zre[:, pp * lanes:(pp + 1) * lanes] = jnp.dot(up, bre_ref[pp], preferred_element_type=F32)
        zim[:, pp * lanes:(pp + 1) * lanes] = jnp.dot(up, bim_ref[pp], preferred_element_type=F32)
    ar = ar_ref[0]
    ai = ai_ref[0]

    def step(c, carry):
        new = []
        for b in range(n_batch):
            sr, si = carry[2 * b], carry[2 * b + 1]
            row = pl.ds(b * chunks_per_batch + c, 1)
            pre[row, :] = sr
            pim[row, :] = si
            new.append(ar * sr - ai * si + zre[row, :])
            new.append(ar * si + ai * sr + zim[row, :])
        return tuple(new)

    zero = jnp.zeros((1, zre.shape[1]), F32)
    lax.fori_loop(0, chunks_per_batch, step, (zero,) * (2 * n_batch), unroll=8)

    half = o_ref.shape[2]
    for pp in range(npairs):
        sl = slice(pp * lanes, (pp + 1) * lanes)
        yp = jnp.dot(pre[:, sl].astype(BF16), cre_ref[pp], preferred_element_type=F32)
        yp += jnp.dot(pim[:, sl].astype(BF16), cim_ref[pp], preferred_element_type=F32)
        for gg in range(2):
            gi = 2 * pp + gg
            y = yp[:, gg * half:(gg + 1) * half]
            y += jnp.dot(u_ref[gi], t_ref[gi], preferred_element_type=F32)
            o_ref[gi] = jax.nn.gelu(y).astype(o_ref.dtype)


def _s5_core(u, ops, n_batch):
    t_op, bre, bim, cre, cim, a16r, a16i = ops
    t, w = u.shape
    g = t_op.shape[0]
    h = w // g
    j = S5_CHUNK
    nchunk = t // j
    gb = S5_GROUP_BLOCK
    ug = u.reshape(nchunk, j, g, h).transpose(2, 0, 1, 3).reshape(g, nchunk, j * h)
    p2 = bre.shape[2]
    yg = pl.pallas_call(
        functools.partial(_s5_kernel, chunks_per_batch=nchunk // n_batch, n_batch=n_batch),
        grid=(g // gb,),
        in_specs=[pl.BlockSpec((gb, nchunk, j * h), lambda i: (i, 0, 0)),
                  pl.BlockSpec((gb, j * h, j * h), lambda i: (i, 0, 0)),
                  pl.BlockSpec((gb // 2, 2 * j * h, p2), lambda i: (i, 0, 0)),
                  pl.BlockSpec((gb // 2, 2 * j * h, p2), lambda i: (i, 0, 0)),
                  pl.BlockSpec((gb // 2, p2, 2 * j * h), lambda i: (i, 0, 0)),
                  pl.BlockSpec((gb // 2, p2, 2 * j * h), lambda i: (i, 0, 0)),
                  pl.BlockSpec((1, 1, gb // 2 * p2), lambda i: (i, 0, 0)),
                  pl.BlockSpec((1, 1, gb // 2 * p2), lambda i: (i, 0, 0))],
        out_specs=pl.BlockSpec((gb, nchunk, j * h), lambda i: (i, 0, 0)),
        out_shape=jax.ShapeDtypeStruct((g, nchunk, j * h), BF16),
        scratch_shapes=[pltpu.VMEM((nchunk, gb // 2 * p2), F32)] * 4,
        compiler_params=_params(("parallel",)),
        name="s5_chunked_scan",
    )(ug, t_op, bre, bim, cre, cim, a16r, a16i)
    return yg.reshape(g, nchunk, j, h).transpose(1, 2, 0, 3).reshape(t, w)


def _glu_norm_kernel(g_ref, w_ref, gn_ref, o_ref):
    g = g_ref[...]
    gate = jax.nn.sigmoid(jnp.dot(g, w_ref[...], preferred_element_type=F32))
    y = g.astype(F32) * gate
    y = y * lax.rsqrt(jnp.mean(y * y, axis=-1, keepdims=True) + EPS)
    o_ref[...] = (y * gn_ref[...]).astype(o_ref.dtype)


def _glu_norm(g, w_glu, gn, tm=512):
    t, w = g.shape
    return pl.pallas_call(
        _glu_norm_kernel,
        grid=(t // tm,),
        in_specs=[pl.BlockSpec((tm, w), lambda i: (i, 0)),
                  pl.BlockSpec((w, w), lambda i: (0, 0)),
                  pl.BlockSpec((1, w), lambda i: (0, 0))],
        out_specs=pl.BlockSpec((tm, w), lambda i: (i, 0)),
        out_shape=jax.ShapeDtypeStruct((t, w), BF16),
        compiler_params=_params(("parallel",)),
        name="glu_groupnorm",
    )(g, w_glu, gn.astype(F32).reshape(1, w))


def _attn_kernel(sink_ref, q_ref, kp_ref, kc_ref, vp_ref, vc_ref, qg_ref, kg_ref, gn_ref, o_ref,
                 acc, *, n_q_heads, n_kv_heads, head_dim):
    nblk = pl.program_id(1)
    w = WINDOW
    grp = n_q_heads // n_kv_heads
    t_loc = lax.broadcasted_iota(jnp.int32, (w, 2 * w), 0)
    s_loc = lax.broadcasted_iota(jnp.int32, (w, 2 * w), 1)
    dist = t_loc + w - s_loc
    valid = (dist >= 0) & (dist < w) & (s_loc + nblk * w >= w)
    distf = dist.astype(F32)
    qg = qg_ref[...]
    kg = kg_ref[...]

    def headnorm(v, gain):
        v = v.astype(F32)
        return v * lax.rsqrt(jnp.mean(v * v, axis=-1, keepdims=True) + EPS) * gain

    for hk in range(n_kv_heads):
        ksl = slice(hk * head_dim, (hk + 1) * head_dim)
        k = jnp.concatenate([kp_ref[:, ksl], kc_ref[:, ksl]], axis=0)
        v = jnp.concatenate([vp_ref[:, ksl], vc_ref[:, ksl]], axis=0)
        kn = headnorm(k, kg).astype(BF16)
        for gq in range(grp):
            hq = hk * grp + gq
            qsl = slice(hq * head_dim, (hq + 1) * head_dim)
            qn = headnorm(q_ref[:, qsl], qg).astype(BF16)
            s = lax.dot_general(qn, kn, (((1,), (1,)), ((), ())), preferred_element_type=F32)
            slope = 2.0 ** (-8.0 * (hq + 1) / n_q_heads)
            s = s * (head_dim ** -0.5) - slope * distf
            s = jnp.where(valid, s, -jnp.inf)
            sink = sink_ref[hq]
            m = jnp.maximum(jnp.max(s, axis=-1, keepdims=True), sink)
            e = jnp.exp(s - m)
            denom = jnp.sum(e, axis=-1, keepdims=True) + jnp.exp(sink - m)
            pv = jnp.dot(e.astype(BF16), v, preferred_element_type=F32)
            acc[:, qsl] = pv / denom
    y = acc[...]
    y = y * lax.rsqrt(jnp.mean(y * y, axis=-1, keepdims=True) + EPS)
    o_ref[...] = (y * gn_ref[...]).astype(o_ref.dtype)


def _attention(z, ssm_w, attn_w, kv_w, q_gain, k_gain, sinks, gn, n_batch):
    t = z.shape[0]
    w = WINDOW
    head_dim = q_gain.shape[0]
    n_q = attn_w // head_dim
    n_kv = kv_w // head_dim
    nb = t // n_batch // w
    qcol = ssm_w // attn_w
    kcol = (ssm_w + attn_w) // kv_w
    vcol = kcol + 1
    cur = lambda col: (lambda b, n: (b * nb + n, col))
    prev = lambda col: (lambda b, n: (b * nb + jnp.maximum(n - 1, 0), col))
    vec = lambda width: pl.BlockSpec((1, width), lambda b, n: (0, 0))
    return pl.pallas_call(
        functools.partial(_attn_kernel, n_q_heads=n_q, n_kv_heads=n_kv, head_dim=head_dim),
        grid=(n_batch, nb),
        in_specs=[pl.BlockSpec(memory_space=pltpu.SMEM),
                  pl.BlockSpec((w, attn_w), cur(qcol)),
                  pl.BlockSpec((w, kv_w), prev(kcol)),
                  pl.BlockSpec((w, kv_w), cur(kcol)),
                  pl.BlockSpec((w, kv_w), prev(vcol)),
                  pl.BlockSpec((w, kv_w), cur(vcol)),
                  vec(head_dim), vec(head_dim), vec(attn_w)],
        out_specs=pl.BlockSpec((w, attn_w), lambda b, n: (b * nb + n, 0)),
        out_shape=jax.ShapeDtypeStruct((t, attn_w), BF16),
        scratch_shapes=[pltpu.VMEM((w, attn_w), F32)],
        compiler_params=_params(("parallel", "parallel")),
        name="swa_attention",
    )(sinks.astype(F32), z, z, z, z, z,
      q_gain.astype(F32).reshape(1, head_dim), k_gain.astype(F32).reshape(1, head_dim),
      gn.astype(F32).reshape(1, attn_w))


def _kth_largest_rows(vals, k):
    out = []
    work = vals
    for _ in range(k):
        m = jnp.max(work, axis=0, keepdims=True)
        out.append(m)
        work = jnp.where(work == m, -jnp.inf, work)
    return out


def _peer_topk_kernel(qt_ref, k1_ref, k2_ref, s1_ref, s2_ref, st_ref, *, n_heads, n_keys):
    half = k1_ref.shape[1]
    hp = lax.Precision.HIGHEST
    k1 = k1_ref[...]
    k2 = k2_ref[...]
    taus, c1s, m2s = [], [], []
    for h in range(n_heads):
        base = h * 2 * half
        s1 = jnp.dot(k1, qt_ref[base:base + half, :], preferred_element_type=F32, precision=hp)
        s2 = jnp.dot(k2, qt_ref[base + half:base + 2 * half, :], preferred_element_type=F32,
                     precision=hp)
        s1_ref[h * n_keys:(h + 1) * n_keys, :] = s1
        s2_ref[h * n_keys:(h + 1) * n_keys, :] = s2
        v1 = _kth_largest_rows(s1, PEER_TOPK)
        v2 = _kth_largest_rows(s2, PEER_TOPK)
        v2m = jnp.concatenate(v2, axis=0)
        cand = jnp.concatenate([v1[i] + v2m for i in range(PEER_TOPK)], axis=0)
        tau = _kth_largest_rows(cand, PEER_TOPK)[-1]
        top = v1[0] + v2[0]
        zsum = jnp.sum(jnp.where(cand >= tau, jnp.exp(cand - top), 0.0), axis=0, keepdims=True)
        taus.append(tau)
        c1s.append(v1[0] + jnp.log(zsum))
        m2s.append(v2[0])
    pad = jnp.zeros((st_ref.shape[0] - 3 * n_heads, st_ref.shape[1]), F32)
    st_ref[...] = jnp.concatenate(taus + c1s + m2s + [pad], axis=0)


def _peer_topk(qt, k1, k2, tn=256):
    hq, t = qt.shape
    n_keys, half = k1.shape
    n_heads = hq // (2 * half)
    srows = n_heads * n_keys
    col = lambda rows: pl.BlockSpec((rows, tn), lambda i: (0, i))
    return pl.pallas_call(
        functools.partial(_peer_topk_kernel, n_heads=n_heads, n_keys=n_keys),
        grid=(t // tn,),
        in_specs=[col(hq),
                  pl.BlockSpec((n_keys, half), lambda i: (0, 0)),
                  pl.BlockSpec((n_keys, half), lambda i: (0, 0))],
        out_specs=[col(srows), col(srows), col(32)],
        out_shape=[jax.ShapeDtypeStruct((srows, t), F32), jax.ShapeDtypeStruct((srows, t), F32),
                   jax.ShapeDtypeStruct((32, t), F32)],
        compiler_params=_params(("parallel",)),
        name="peer_topk",
    )(qt, k1.astype(F32), k2.astype(F32))


def _peer_dense_kernel(h_ref, u_ref, v_ref, s1_ref, s2_ref, st_ref, o_ref, e1, e2, abuf, gbuf,
                       *, n_heads, n_keys):
    j = pl.program_id(1)
    te = u_ref.shape[0]
    d = v_ref.shape[1]
    na = te // n_keys
    rb = 32
    dc = 512

    @pl.when(j == 0)
    def _():
        o_ref[...] = jnp.zeros_like(o_ref)
        for h in range(n_heads):
            rows = slice(h * n_keys, (h + 1) * n_keys)
            e1[rows, :] = jnp.exp(s1_ref[rows, :] - st_ref[n_heads + h:n_heads + h + 1, :])
            e2[rows, :] = jnp.exp(s2_ref[rows, :] - st_ref[2 * n_heads + h:2 * n_heads + h + 1, :])

    act = lax.dot_general(u_ref[...], h_ref[...], (((1,), (1,)), ((), ())),
                          preferred_element_type=F32)
    abuf[...] = jax.nn.gelu(act)
    for aa in range(na):
        a = j * na + aa
        for b0 in range(0, n_keys, rb):
            wgt = jnp.zeros((rb, abuf.shape[1]), F32)
            for h in range(n_heads):
                r1 = pl.ds(h * n_keys + a, 1)
                r2 = slice(h * n_keys + b0, h * n_keys + b0 + rb)
                ssum = s1_ref[r1, :] + s2_ref[r2, :]
                wgt += jnp.where(ssum >= st_ref[h:h + 1, :], e1[r1, :] * e2[r2, :], 0.0)
            rows = slice(aa * n_keys + b0, aa * n_keys + b0 + rb)
            gbuf[rows, :] = (wgt * abuf[rows, :]).astype(gbuf.dtype)
    g = gbuf[...]
    for c0 in range(0, d, dc):
        o_ref[:, c0:c0 + dc] += lax.dot_general(g, v_ref[:, c0:c0 + dc], (((0,), (0,)), ((), ())),
                                                preferred_element_type=F32)


def _peer_dense(h2, u_tab, v_tab, s1t, s2t, stats, tm=512, te=256):
    t, d = h2.shape
    ne = u_tab.shape[0]
    n_keys = int(round(math.sqrt(ne)))
    n_heads = s1t.shape[0] // n_keys
    srows = s1t.shape[0]
    tok = lambda rows: pl.BlockSpec((rows, tm), lambda i, j: (0, i))
    return pl.pallas_call(
        functools.partial(_peer_dense_kernel, n_heads=n_heads, n_keys=n_keys),
        grid=(t // tm, ne // te),
        in_specs=[pl.BlockSpec((tm, d), lambda i, j: (i, 0)),
                  pl.BlockSpec((te, d), lambda i, j: (j, 0)),
                  pl.BlockSpec((te, d), lambda i, j: (j, 0)),
                  tok(srows), tok(srows), tok(32)],
        out_specs=pl.BlockSpec((tm, d), lambda i, j: (i, 0)),
        out_shape=jax.ShapeDtypeStruct((t, d), F32),
        scratch_shapes=[pltpu.VMEM((srows, tm), F32), pltpu.VMEM((srows, tm), F32),
                        pltpu.VMEM((te, tm), F32), pltpu.VMEM((te, tm), BF16)],
        compiler_params=_params(("parallel", "arbitrary")),
        name="peer_dense",
    )(h2, u_tab, v_tab, s1t, s2t, stats)


def kernel(x, c, w_ada, b_ada, ada_layer, norm1_g, norm2_g, w_in, lam_re, lam_im, log_dt, b_re, b_im,
           c_re, c_im, d_skip, w_glu, q_gain, k_gain, sinks, gn_ssm, gn_attn, w_out, peer_wq, peer_k1,
           peer_k2, peer_u, peer_v):
    bsz, seq, d = x.shape
    t = bsz * seq
    depth = w_in.shape[0]
    n_mod = ada_layer.shape[1]
    ssm_w = w_glu.shape[1]
    attn_w = gn_attn.shape[1]
    kv_w = (w_in.shape[2] - ssm_w - attn_w) // 2

    cond = _cond(c, w_ada, b_ada).reshape(bsz, n_mod, d)
    xf = x.astype(F32).reshape(t, d)
    delta, gate_prev = None, None
    for l in range(depth):
        mod = cond + ada_layer[l].astype(F32)
        shift1, scale1, gate1, shift2, scale2, gate2 = (mod[:, i] for i in range(n_mod))

        xf, h = _adaln_norm(xf, delta, gate_prev, norm1_g[l], scale1, shift1, seq)
        z = _matmul(h, w_in[l].astype(BF16), BF16, tm=1024, tn=512)
        ops = _s5_operators(lam_re[l], lam_im[l], log_dt[l], b_re[l], b_im[l], c_re[l], c_im[l],
                            d_skip[l])
        g = _s5_core(z[:, :ssm_w], ops, bsz)
        y_ssm = _glu_norm(g, w_glu[l].astype(BF16), gn_ssm[l])
        y_attn = _attention(z, ssm_w, attn_w, kv_w, q_gain[l], k_gain[l], sinks[l], gn_attn[l], bsz)
        mixed = _matmul_concat(y_ssm, y_attn, w_out[l].astype(BF16), F32, tm=1024, tn=512)

        xf, h2 = _adaln_norm(xf, mixed, gate1, norm2_g[l], scale2, shift2, seq)
        qt = _matmul_nt(peer_wq[l].T.astype(BF16), h2, F32, tn=512)
        s1t, s2t, stats = _peer_topk(qt, peer_k1[l], peer_k2[l])
        delta = _peer_dense(h2, peer_u[l].astype(BF16), peer_v[l].astype(BF16), s1t, s2t, stats)
        gate_prev = gate2
    out = _residual_add(xf, delta, gate_prev, seq)
    return out.reshape(bsz, seq, d).astype(x.dtype)
```

```python
import functools
import math

import jax
import jax.numpy as jnp
from jax import lax
from jax.experimental import pallas as pl
from jax.experimental.pallas import tpu as pltpu

F32 = jnp.float32
BF16 = jnp.bfloat16
EPS = 1e-6
WINDOW = 128
PEER_TOPK = 16
S5_CHUNK = 16
S5_GROUP_BLOCK = 8
V7X_VMEM_LIMIT = 56 * 1024 * 1024


def _params(semantics, vmem=V7X_VMEM_LIMIT, flags=None):
    return pltpu.CompilerParams(dimension_semantics=semantics, vmem_limit_bytes=vmem, flags=flags)


def _mm_kernel(a_ref, b_ref, o_ref):
    o_ref[...] = jnp.dot(a_ref[...], b_ref[...], preferred_element_type=F32).astype(o_ref.dtype)


def _matmul(a, b, out_dtype, tm, tn):
    m, k = a.shape
    n = b.shape[1]
    return pl.pallas_call(
        _mm_kernel,
        grid=(m // tm, n // tn),
        in_specs=[pl.BlockSpec((tm, k), lambda i, j: (i, 0)),
                  pl.BlockSpec((k, tn), lambda i, j: (0, j))],
        out_specs=pl.BlockSpec((tm, tn), lambda i, j: (i, j)),
        out_shape=jax.ShapeDtypeStruct((m, n), out_dtype),
        compiler_params=_params(("parallel", "parallel")),
        name="matmul",
    )(a, b)


def _mm_cb_kernel(a_ref, b_ref, o_ref):
    res = jnp.dot(a_ref[...], b_ref[...], preferred_element_type=F32).astype(o_ref.dtype)
    lanes = o_ref.shape[2]
    for k in range(o_ref.shape[0]):
        o_ref[k] = res[:, k * lanes:(k + 1) * lanes]


def _matmul_channel_blocks(a, b, out_dtype, tm, tn, lanes=128):
    m, k = a.shape
    n = b.shape[1]
    return pl.pallas_call(
        _mm_cb_kernel,
        grid=(m // tm, n // tn),
        in_specs=[pl.BlockSpec((tm, k), lambda i, j: (i, 0)),
                  pl.BlockSpec((k, tn), lambda i, j: (0, j))],
        out_specs=pl.BlockSpec((tn // lanes, tm, lanes), lambda i, j: (j, i, 0)),
        out_shape=jax.ShapeDtypeStruct((n // lanes, m, lanes), out_dtype),
        compiler_params=_params(("parallel", "parallel")),
        name="matmul_channel_blocks",
    )(a, b)


def _mm2_kernel(a1_ref, a2_ref, b1_ref, b2_ref, o_ref):
    acc = jnp.dot(a1_ref[...], b1_ref[...], preferred_element_type=F32)
    acc += jnp.dot(a2_ref[...], b2_ref[...], preferred_element_type=F32)
    o_ref[...] = acc.astype(o_ref.dtype)


def _matmul_concat(a1, a2, b, out_dtype, tm, tn):
    m, k1 = a1.shape
    k2 = a2.shape[1]
    assert k1 == k2
    n = b.shape[1]
    return pl.pallas_call(
        _mm2_kernel,
        grid=(m // tm, n // tn),
        in_specs=[pl.BlockSpec((tm, k1), lambda i, j: (i, 0)),
                  pl.BlockSpec((tm, k2), lambda i, j: (i, 0)),
                  pl.BlockSpec((k1, tn), lambda i, j: (0, j)),
                  pl.BlockSpec((k2, tn), lambda i, j: (1, j))],
        out_specs=pl.BlockSpec((tm, tn), lambda i, j: (i, j)),
        out_shape=jax.ShapeDtypeStruct((m, n), out_dtype),
        compiler_params=_params(("parallel", "parallel")),
        name="matmul_concat",
    )(a1, a2, b, b)


def _mm_nt_kernel(a_ref, b_ref, o_ref):
    o_ref[...] = lax.dot_general(a_ref[...], b_ref[...], (((1,), (1,)), ((), ())),
                                 preferred_element_type=F32).astype(o_ref.dtype)


def _matmul_nt(a, b, out_dtype, tn):
    m, k = a.shape
    n = b.shape[0]
    return pl.pallas_call(
        _mm_nt_kernel,
        grid=(n // tn,),
        in_specs=[pl.BlockSpec((m, k), lambda j: (0, 0)),
                  pl.BlockSpec((tn, k), lambda j: (j, 0))],
        out_specs=pl.BlockSpec((m, tn), lambda j: (0, j)),
        out_shape=jax.ShapeDtypeStruct((m, n), out_dtype),
        compiler_params=_params(("parallel",)),
        name="matmul_nt",
    )(a, b)


def _cond_kernel(c_ref, w_ref, b_ref, o_ref):
    c = c_ref[...]
    s = c * jax.nn.sigmoid(c)
    o_ref[...] = jnp.dot(s, w_ref[...], preferred_element_type=F32,
                         precision=lax.Precision.HIGHEST) + b_ref[...]


def _cond(c, w_ada, b_ada, tn=512):
    bsz, d = c.shape
    n = w_ada.shape[1]
    rows = 8
    cp = jnp.zeros((rows, d), F32).at[:bsz].set(c.astype(F32))
    out = pl.pallas_call(
        _cond_kernel,
        grid=(n // tn,),
        in_specs=[pl.BlockSpec((rows, d), lambda j: (0, 0)),
                  pl.BlockSpec((d, tn), lambda j: (0, j)),
                  pl.BlockSpec((1, tn), lambda j: (0, j))],
        out_specs=pl.BlockSpec((rows, tn), lambda j: (0, j)),
        out_shape=jax.ShapeDtypeStruct((rows, n), F32),
        compiler_params=_params(("parallel",)),
        name="adaln_cond",
    )(cp, w_ada.astype(F32), b_ada.astype(F32).reshape(1, n))
    return out[:bsz]


def _norm_body(x, g_ref, scale_ref, shift_ref, h_ref):
    y = x * lax.rsqrt(jnp.mean(x * x, axis=-1, keepdims=True) + EPS)
    y = y * g_ref[...]
    h_ref[...] = (y * (1.0 + scale_ref[0]) + shift_ref[0]).astype(h_ref.dtype)


def _norm_kernel(x_ref, g_ref, scale_ref, shift_ref, h_ref):
    _norm_body(x_ref[...], g_ref, scale_ref, shift_ref, h_ref)


def _resnorm_kernel(x_ref, d_ref, gate_ref, g_ref, scale_ref, shift_ref, xo_ref, h_ref):
    x = x_ref[...] + gate_ref[0] * d_ref[...]
    xo_ref[...] = x
    _norm_body(x, g_ref, scale_ref, shift_ref, h_ref)


def _adaln_norm(x, delta, gate, g, scale, shift, rows_per_batch, tm=256):
    t, d = x.shape
    tpb = rows_per_batch // tm
    row = pl.BlockSpec((tm, d), lambda i: (i, 0))
    per_batch = pl.BlockSpec((1, 1, d), lambda i: (i // tpb, 0, 0))
    vec = pl.BlockSpec((1, d), lambda i: (0, 0))
    g2 = g.astype(F32).reshape(1, d)
    b3 = lambda v: v.astype(F32).reshape(v.shape[0], 1, d)
    if delta is None:
        h = pl.pallas_call(
            _norm_kernel,
            grid=(t // tm,),
            in_specs=[row, vec, per_batch, per_batch],
            out_specs=row,
            out_shape=jax.ShapeDtypeStruct((t, d), BF16),
            compiler_params=_params(("parallel",)),
            name="adaln_norm",
        )(x, g2, b3(scale), b3(shift))
        return x, h
    return pl.pallas_call(
        _resnorm_kernel,
        grid=(t // tm,),
        in_specs=[row, row, per_batch, vec, per_batch, per_batch],
        out_specs=[row, row],
        out_shape=[jax.ShapeDtypeStruct((t, d), F32), jax.ShapeDtypeStruct((t, d), BF16)],
        compiler_params=_params(("parallel",)),
        name="residual_adaln_norm",
    )(x, delta, b3(gate), g2, b3(scale), b3(shift))


def _residual_kernel(x_ref, d_ref, gate_ref, o_ref):
    o_ref[...] = x_ref[...] + gate_ref[0] * d_ref[...]


def _residual_add(x, delta, gate, rows_per_batch, tm=256):
    t, d = x.shape
    tpb = rows_per_batch // tm
    row = pl.BlockSpec((tm, d), lambda i: (i, 0))
    return pl.pallas_call(
        _residual_kernel,
        grid=(t // tm,),
        in_specs=[row, row, pl.BlockSpec((1, 1, d), lambda i: (i // tpb, 0, 0))],
        out_specs=row,
        out_shape=jax.ShapeDtypeStruct((t, d), F32),
        compiler_params=_params(("parallel",)),
        name="residual_add",
    )(x, delta, gate.astype(F32).reshape(gate.shape[0], 1, d))


def _s5_operators(lam_re, lam_im, log_dt, b_re, b_im, c_re, c_im, d_skip):
    hp = lax.Precision.HIGHEST
    j = S5_CHUNK
    f = lambda v: v.astype(F32)
    lr, li = f(lam_re), f(lam_im)
    g, p = lr.shape
    h = b_re.shape[-1]
    dt = jnp.exp(f(log_dt))[:, None]
    mag = jnp.exp(lr * dt)
    ar = mag * jnp.cos(li * dt)
    ai = mag * jnp.sin(li * dt)
    den = lr * lr + li * li
    kr = ((ar - 1.0) * lr + ai * li) / den
    ki = (ai * lr - (ar - 1.0) * li) / den
    br, bi = f(b_re), f(b_im)
    bbr = kr[..., None] * br - ki[..., None] * bi
    bbi = kr[..., None] * bi + ki[..., None] * br
    cr, ci = f(c_re), f(c_im)
    pr, pi = [jnp.ones_like(ar)], [jnp.zeros_like(ar)]
    for _ in range(j):
        pr.append(pr[-1] * ar - pi[-1] * ai)
        pi.append(pr[-2] * ai + pi[-1] * ar)
    pr, pi = jnp.stack(pr), jnp.stack(pi)
    er = pr[:j, :, :, None] * bbr - pi[:j, :, :, None] * bbi
    ei = pr[:j, :, :, None] * bbi + pi[:j, :, :, None] * bbr
    kk = (jnp.einsum('ghp,ngpk->nghk', cr, er, precision=hp)
          - jnp.einsum('ghp,ngpk->nghk', ci, ei, precision=hp))
    kk = kk.at[0].add(f(d_skip)[:, :, None] * jnp.eye(h, dtype=F32))
    lag = jnp.arange(j)[None, :] - jnp.arange(j)[:, None]
    tt = jnp.where((lag >= 0)[:, :, None, None, None], kk[jnp.clip(lag, 0)], 0.0)
    gb = S5_GROUP_BLOCK
    nblk = g // gb
    t_op = tt.reshape(j, j, nblk, gb, h, h).transpose(2, 0, 5, 1, 3, 4).reshape(nblk, j * h, j * gb * h)

    def chunk_in(e):
        return (e[::-1].reshape(j, nblk, gb, p, h).transpose(1, 0, 4, 2, 3)
                .reshape(nblk, j * h, gb * p))

    p1r, p1i = pr[1:], pi[1:]
    cc_r = cr[None] * p1r[:, :, None, :] - ci[None] * p1i[:, :, None, :]
    cc_i = -(cr[None] * p1i[:, :, None, :] + ci[None] * p1r[:, :, None, :])

    def chunk_out(m):
        return (m.reshape(j, nblk, gb, h, p).transpose(1, 4, 0, 2, 3)
                .reshape(nblk, p, j * gb * h))

    a16r = pr[j].reshape(nblk, 1, gb * p)
    a16i = pi[j].reshape(nblk, 1, gb * p)
    return (t_op.astype(BF16), chunk_in(er).astype(BF16), chunk_in(ei).astype(BF16),
            chunk_out(cc_r).astype(BF16), chunk_out(cc_i).astype(BF16), a16r, a16i)


def _s5_kernel(u_ref, t_ref, bre_ref, bim_ref, cre_ref, cim_ref, ar_ref, ai_ref, mt_ref, mb_ref,
               o_ref, tblk, bblk_r, bblk_i, cblk_r, cblk_i, zre, zim, pre, pim,
               *, chunks_per_batch, n_batch):
    gb = S5_GROUP_BLOCK
    j = S5_CHUNK
    hh = t_ref.shape[1] // j
    pp = cre_ref.shape[1]
    lanes = gb * hh
    for i in range(j):
        for gl in range(gb):
            src = slice(i * hh, (i + 1) * hh)
            dst = slice(i * lanes + gl * hh, i * lanes + (gl + 1) * hh)
            tblk[dst, :] = t_ref[0, src, :] * mt_ref[gl]
            bblk_r[dst, :] = bre_ref[0, src, :] * mb_ref[gl]
            bblk_i[dst, :] = bim_ref[0, src, :] * mb_ref[gl]
    for gl in range(gb):
        for r0 in range(0, pp, hh):
            src = slice(r0, r0 + hh)
            dst = slice(gl * pp + r0, gl * pp + r0 + hh)
            cblk_r[dst, :] = cre_ref[0, src, :] * mt_ref[gl]
            cblk_i[dst, :] = cim_ref[0, src, :] * mt_ref[gl]

    u = u_ref[0]
    zre[...] = jnp.dot(u, bblk_r[...], preferred_element_type=F32)
    zim[...] = jnp.dot(u, bblk_i[...], preferred_element_type=F32)
    ar = ar_ref[0]
    ai = ai_ref[0]

    def step(c, carry):
        new = []
        for b in range(n_batch):
            sr, si = carry[2 * b], carry[2 * b + 1]
            row = pl.ds(b * chunks_per_batch + c, 1)
            pre[row, :] = sr
            pim[row, :] = si
            new.append(ar * sr - ai * si + zre[row, :])
            new.append(ar * si + ai * sr + zim[row, :])
        return tuple(new)

    zero = jnp.zeros((1, zre.shape[1]), F32)
    lax.fori_loop(0, chunks_per_batch, step, (zero,) * (2 * n_batch), unroll=8)

    pr_b = pre[...].astype(BF16)
    pi_b = pim[...].astype(BF16)
    ncol = 4 * lanes
    for c0 in range(0, j * lanes, ncol):
        cols = slice(c0, c0 + ncol)
        k_rows = c0 + ncol
        y = jnp.dot(u[:, :k_rows], tblk[:k_rows, cols], preferred_element_type=F32)
        y += jnp.dot(pr_b, cblk_r[:, cols], preferred_element_type=F32)
        y += jnp.dot(pi_b, cblk_i[:, cols], preferred_element_type=F32)
        o_ref[0, :, cols] = jax.nn.gelu(y).astype(o_ref.dtype)


def _s5_core(ub, ops, n_batch):
    t_op, bre, bim, cre, cim, a16r, a16i = ops
    nblk, t, lanes = ub.shape
    j = S5_CHUNK
    gb = S5_GROUP_BLOCK
    hh = lanes // gb
    nchunk = t // j
    pp = cre.shape[1]
    uc = ub.reshape(nblk, nchunk, j * lanes)
    def own(width, per_group):
        col_group = jnp.arange(width) // per_group % gb
        m = col_group[None, None, :] == jnp.arange(gb)[:, None, None]
        return jnp.broadcast_to(m, (gb, hh, width)).astype(BF16)

    mask_t = own(j * lanes, hh)
    mask_b = own(gb * pp, pp)
    blk = lambda r, c: pl.BlockSpec((1, r, c), lambda i: (i, 0, 0))
    const = lambda a: pl.BlockSpec(a.shape, lambda i: (0, 0, 0))
    yc = pl.pallas_call(
        functools.partial(_s5_kernel, chunks_per_batch=nchunk // n_batch, n_batch=n_batch),
        grid=(nblk,),
        in_specs=[blk(nchunk, j * lanes), blk(j * hh, j * lanes),
                  blk(j * hh, gb * pp), blk(j * hh, gb * pp),
                  blk(pp, j * lanes), blk(pp, j * lanes),
                  blk(1, gb * pp), blk(1, gb * pp), const(mask_t), const(mask_b)],
        out_specs=blk(nchunk, j * lanes),
        out_shape=jax.ShapeDtypeStruct((nblk, nchunk, j * lanes), BF16),
        scratch_shapes=[pltpu.VMEM((j * lanes, j * lanes), BF16),
                        pltpu.VMEM((j * lanes, gb * pp), BF16), pltpu.VMEM((j * lanes, gb * pp), BF16),
                        pltpu.VMEM((gb * pp, j * lanes), BF16), pltpu.VMEM((gb * pp, j * lanes), BF16)]
                       + [pltpu.VMEM((nchunk, gb * pp), F32)] * 4,
        compiler_params=_params(("parallel",)),
        name="s5_chunked_scan",
    )(uc, t_op, bre, bim, cre, cim, a16r, a16i, mask_t, mask_b)
    return yc.reshape(nblk, t, lanes)


def _glu_norm_kernel(g_ref, w_ref, gn_ref, o_ref):
    g = jnp.concatenate([g_ref[i] for i in range(g_ref.shape[0])], axis=1)
    gate = jax.nn.sigmoid(jnp.dot(g, w_ref[...], preferred_element_type=F32))
    y = g.astype(F32) * gate
    y = y * lax.rsqrt(jnp.mean(y * y, axis=-1, keepdims=True) + EPS)
    o_ref[...] = (y * gn_ref[...]).astype(o_ref.dtype)


def _glu_norm(gb_, w_glu, gn, tm=512):
    nblk, t, lanes = gb_.shape
    w = nblk * lanes
    return pl.pallas_call(
        _glu_norm_kernel,
        grid=(t // tm,),
        in_specs=[pl.BlockSpec((nblk, tm, lanes), lambda i: (0, i, 0)),
                  pl.BlockSpec((w, w), lambda i: (0, 0)),
                  pl.BlockSpec((1, w), lambda i: (0, 0))],
        out_specs=pl.BlockSpec((tm, w), lambda i: (i, 0)),
        out_shape=jax.ShapeDtypeStruct((t, w), BF16),
        compiler_params=_params(("parallel",)),
        name="glu_groupnorm",
    )(gb_, w_glu, gn.astype(F32).reshape(1, w))


def _head_rms_scale(x, sel_ref, exp_ref, head_dim):
    ss = jnp.dot((x * x).astype(BF16), sel_ref[...], preferred_element_type=F32)
    r = lax.rsqrt(ss * (1.0 / head_dim) + EPS)
    r_hi = r.astype(BF16)
    r_lo = (r - r_hi.astype(F32)).astype(BF16)
    return (jnp.dot(r_hi, exp_ref[...], preferred_element_type=F32)
            + jnp.dot(r_lo, exp_ref[...], preferred_element_type=F32))


def _attn_kernel(sink_ref, q_ref, kp_ref, kc_ref, vp_ref, vc_ref, qg_ref, kg_ref, gn_ref,
                 qsel_ref, qexp_ref, ksel_ref, kexp_ref, o_ref, acc,
                 *, n_q_heads, n_kv_heads, head_dim):
    nblk = pl.program_id(1)
    w = WINDOW
    lw = 2 * head_dim
    grp = n_q_heads // n_kv_heads
    t_loc = lax.broadcasted_iota(jnp.int32, (w, 2 * w), 0)
    s_loc = lax.broadcasted_iota(jnp.int32, (w, 2 * w), 1)
    dist = t_loc + w - s_loc
    valid = (dist >= 0) & (dist < w) & (s_loc + nblk * w >= w)
    dmask = jnp.where(valid, dist.astype(F32), jnp.inf)

    q = q_ref[...].astype(F32)
    qn = (q * _head_rms_scale(q, qsel_ref, qexp_ref, head_dim) * qg_ref[...]).astype(BF16)
    k = jnp.concatenate([kp_ref[...], kc_ref[...]], axis=0).astype(F32)
    kn = k * _head_rms_scale(k, ksel_ref, kexp_ref, head_dim) * kg_ref[...]
    v = jnp.concatenate([vp_ref[...], vc_ref[...]], axis=0).astype(F32)

    lane = lax.broadcasted_iota(jnp.int32, (2 * w, lw), 1)
    lo = lane < head_dim

    def block_diag(x, hk):
        grp_lanes = x[:, (hk // 2) * lw:(hk // 2 + 1) * lw]
        swapped = pltpu.roll(grp_lanes, head_dim, 1)
        low, high = (grp_lanes, swapped) if hk % 2 == 0 else (swapped, grp_lanes)
        return jnp.concatenate([jnp.where(lo, low, 0.0), jnp.where(lo, 0.0, high)],
                               axis=0).astype(BF16)

    lane_q = lax.broadcasted_iota(jnp.int32, (w, lw), 1) < head_dim
    for hk in range(n_kv_heads):
        kb = block_diag(kn, hk)
        vb = block_diag(v, hk)
        for pq in range(grp // 2):
            pair = hk * (grp // 2) + pq
            cols = slice(pair * lw, (pair + 1) * lw)
            s = lax.dot_general(qn[:, cols], kb, (((1,), (1,)), ((), ())),
                                preferred_element_type=F32)
            es, inv = [], []
            for i in range(2):
                hq = 2 * pair + i
                slope = 2.0 ** (-8.0 * (hq + 1) / n_q_heads)
                si = s[:, i * 2 * w:(i + 1) * 2 * w] - slope * dmask
                sink = sink_ref[hq]
                m = jnp.maximum(jnp.max(si, axis=-1, keepdims=True), sink)
                e = jnp.exp(si - m)
                inv.append(1.0 / (jnp.sum(e, axis=-1, keepdims=True) + jnp.exp(sink - m)))
                es.append(e.astype(BF16))
            pv = jnp.dot(jnp.concatenate(es, axis=1), vb, preferred_element_type=F32)
            acc[:, cols] = pv * jnp.where(lane_q, inv[0], inv[1])
    y = acc[...]
    y = y * lax.rsqrt(jnp.mean(y * y, axis=-1, keepdims=True) + EPS)
    o_ref[...] = (y * gn_ref[...]).astype(o_ref.dtype)


def _attention(z, ssm_w, attn_w, kv_w, q_gain, k_gain, sinks, gn, n_batch):
    t = z.shape[0]
    w = WINDOW
    head_dim = q_gain.shape[0]
    n_q = attn_w // head_dim
    n_kv = kv_w // head_dim
    nb = t // n_batch // w
    qcol = ssm_w // attn_w
    kcol = (ssm_w + attn_w) // kv_w
    vcol = kcol + 1
    cur = lambda col: (lambda b, n: (b * nb + n, col))
    prev = lambda col: (lambda b, n: (b * nb + jnp.maximum(n - 1, 0), col))
    full = lambda r, c: pl.BlockSpec((r, c), lambda b, n: (0, 0))
    nsel = 128

    def selectors(width):
        sel = (jnp.arange(width)[:, None] // head_dim == jnp.arange(nsel)[None, :]).astype(BF16)
        return sel, sel.T

    qsel, qexp = selectors(attn_w)
    ksel, kexp = selectors(kv_w)
    qg_row = (jnp.tile(q_gain.astype(F32), n_q) * head_dim ** -0.5).reshape(1, attn_w)
    kg_row = jnp.tile(k_gain.astype(F32), n_kv).reshape(1, kv_w)
    return pl.pallas_call(
        functools.partial(_attn_kernel, n_q_heads=n_q, n_kv_heads=n_kv, head_dim=head_dim),
        grid=(n_batch, nb),
        in_specs=[pl.BlockSpec(memory_space=pltpu.SMEM),
                  pl.BlockSpec((w, attn_w), cur(qcol)),
                  pl.BlockSpec((w, kv_w), prev(kcol)),
                  pl.BlockSpec((w, kv_w), cur(kcol)),
                  pl.BlockSpec((w, kv_w), prev(vcol)),
                  pl.BlockSpec((w, kv_w), cur(vcol)),
                  full(1, attn_w), full(1, kv_w), full(1, attn_w),
                  full(attn_w, nsel), full(nsel, attn_w), full(kv_w, nsel), full(nsel, kv_w)],
        out_specs=pl.BlockSpec((w, attn_w), lambda b, n: (b * nb + n, 0)),
        out_shape=jax.ShapeDtypeStruct((t, attn_w), BF16),
        scratch_shapes=[pltpu.VMEM((w, attn_w), F32)],
        compiler_params=_params(("parallel", "parallel")),
        name="swa_attention",
    )(sinks.astype(F32), z, z, z, z, z, qg_row, kg_row, gn.astype(F32).reshape(1, attn_w),
      qsel, qexp, ksel, kexp)


def _kth_largest_rows(vals, k):
    out = []
    work = vals
    for _ in range(k):
        m = jnp.max(work, axis=0, keepdims=True)
        out.append(m)
        work = jnp.where(work == m, -jnp.inf, work)
    return out


def _peer_topk_kernel(qt_ref, k1_ref, k2_ref, s1_ref, s2_ref, st_ref, *, n_heads, n_keys):
    half = k1_ref.shape[1]
    hp = lax.Precision.HIGHEST
    k1 = k1_ref[...]
    k2 = k2_ref[...]
    taus, c1s, m2s = [], [], []
    for h in range(n_heads):
        base = h * 2 * half
        s1 = jnp.dot(k1, qt_ref[base:base + half, :], preferred_element_type=F32, precision=hp)
        s2 = jnp.dot(k2, qt_ref[base + half:base + 2 * half, :], preferred_element_type=F32,
                     precision=hp)
        s1_ref[h * n_keys:(h + 1) * n_keys, :] = s1
        s2_ref[h * n_keys:(h + 1) * n_keys, :] = s2
        v1 = _kth_largest_rows(s1, PEER_TOPK)
        v2 = _kth_largest_rows(s2, PEER_TOPK)
        v2m = jnp.concatenate(v2, axis=0)
        cand = jnp.concatenate([v1[i] + v2m for i in range(PEER_TOPK)], axis=0)
        tau = _kth_largest_rows(cand, PEER_TOPK)[-1]
        top = v1[0] + v2[0]
        zsum = jnp.sum(jnp.where(cand >= tau, jnp.exp(cand - top), 0.0), axis=0, keepdims=True)
        taus.append(tau)
        c1s.append(v1[0] + jnp.log(zsum))
        m2s.append(v2[0])
    pad = jnp.zeros((st_ref.shape[0] - 3 * n_heads, st_ref.shape[1]), F32)
    st_ref[...] = jnp.concatenate(taus + c1s + m2s + [pad], axis=0)


def _peer_topk(qt, k1, k2, tn=256):
    hq, t = qt.shape
    n_keys, half = k1.shape
    n_heads = hq // (2 * half)
    srows = n_heads * n_keys
    col = lambda rows: pl.BlockSpec((rows, tn), lambda i: (0, i))
    return pl.pallas_call(
        functools.partial(_peer_topk_kernel, n_heads=n_heads, n_keys=n_keys),
        grid=(t // tn,),
        in_specs=[col(hq),
                  pl.BlockSpec((n_keys, half), lambda i: (0, 0)),
                  pl.BlockSpec((n_keys, half), lambda i: (0, 0))],
        out_specs=[col(srows), col(srows), col(32)],
        out_shape=[jax.ShapeDtypeStruct((srows, t), F32), jax.ShapeDtypeStruct((srows, t), F32),
                   jax.ShapeDtypeStruct((32, t), F32)],
        compiler_params=_params(("parallel",)),
        name="peer_topk",
    )(qt, k1.astype(F32), k2.astype(F32))


def _peer_dense_kernel(h_ref, u_ref, v_ref, s1_ref, s2_ref, st_ref, o_ref, e2, tau8, s1b, e1b, wt,
                       *, n_heads, n_keys):
    j = pl.program_id(1)
    te = u_ref.shape[0]
    tm = h_ref.shape[0]
    d = v_ref.shape[1]
    na = te // n_keys
    sub = 8
    dc = 1024

    @pl.when(j == 0)
    def _():
        o_ref[...] = jnp.zeros_like(o_ref)
        for h in range(n_heads):
            rows = slice(h * n_keys, (h + 1) * n_keys)
            e2[rows, :] = jnp.exp(s2_ref[rows, :] - st_ref[2 * n_heads + h:2 * n_heads + h + 1, :])
            tau8[h * sub:(h + 1) * sub, :] = jnp.broadcast_to(st_ref[h:h + 1, :], (sub, tm))

    for aa in range(na):
        for h in range(n_heads):
            row = s1_ref[pl.ds(h * n_keys + j * na + aa, 1), :]
            k = (aa * n_heads + h) * sub
            s1b[k:k + sub, :] = jnp.broadcast_to(row, (sub, tm))
            e1b[k:k + sub, :] = jnp.broadcast_to(
                jnp.exp(row - st_ref[n_heads + h:n_heads + h + 1, :]), (sub, tm))

    ax = 2
    link = jnp.zeros((sub, tm), F32)
    for a0 in range(0, na, ax):
        for b0 in range(0, n_keys, 2 * sub):
            acc = [[link, link] for _ in range(ax)]
            for h in range(n_heads):
                t8 = tau8[h * sub:(h + 1) * sub, :]
                r2 = [slice(h * n_keys + b0 + y * sub, h * n_keys + b0 + (y + 1) * sub) for y in range(2)]
                s2v = [s2_ref[r, :] for r in r2]
                e2v = [e2[r, :] for r in r2]
                for x in range(ax):
                    k = ((a0 + x) * n_heads + h) * sub
                    s1v = s1b[k:k + sub, :]
                    e1v = e1b[k:k + sub, :]
                    for y in range(2):
                        acc[x][y] = acc[x][y] + jnp.where(s1v + s2v[y] >= t8, e1v * e2v[y], 0.0)
            for x in range(ax):
                r0 = (a0 + x) * n_keys + b0
                wt[r0:r0 + 2 * sub, :] = jnp.concatenate(acc[x], axis=0).astype(wt.dtype)
            link = jnp.minimum(pltpu.roll(acc[0][0], 1, 1), 0.0)

    w = wt[...].T
    act = lax.dot_general(h_ref[...], u_ref[...], (((1,), (1,)), ((), ())),
                          preferred_element_type=F32)
    g = jax.nn.gelu(act).astype(BF16) * w
    for c0 in range(0, d, dc):
        o_ref[:, c0:c0 + dc] += jnp.dot(g, v_ref[:, c0:c0 + dc], preferred_element_type=F32)


def _peer_dense(h2, u_tab, v_tab, s1t, s2t, stats, tm=512, te=512):
    t, d = h2.shape
    ne = u_tab.shape[0]
    n_keys = int(round(math.sqrt(ne)))
    n_heads = s1t.shape[0] // n_keys
    srows = s1t.shape[0]
    na = te // n_keys
    once = pl.Buffered(1)
    tok = lambda rows: pl.BlockSpec((rows, tm), lambda i, j: (0, i), pipeline_mode=once)
    return pl.pallas_call(
        functools.partial(_peer_dense_kernel, n_heads=n_heads, n_keys=n_keys),
        grid=(t // tm, ne // te),
        in_specs=[pl.BlockSpec((tm, d), lambda i, j: (i, 0), pipeline_mode=once),
                  pl.BlockSpec((te, d), lambda i, j: (j, 0)),
                  pl.BlockSpec((te, d), lambda i, j: (j, 0)),
                  tok(srows), tok(srows), tok(32)],
        out_specs=pl.BlockSpec((tm, d), lambda i, j: (i, 0)),
        out_shape=jax.ShapeDtypeStruct((t, d), F32),
        scratch_shapes=[pltpu.VMEM((srows, tm), F32),
                        pltpu.VMEM((n_heads * 8, tm), F32),
                        pltpu.VMEM((na * n_heads * 8, tm), F32),
                        pltpu.VMEM((na * n_heads * 8, tm), F32),
                        pltpu.VMEM((te, tm), BF16)],
        compiler_params=_params(("parallel", "arbitrary")),
        name="peer_dense",
    )(h2, u_tab, v_tab, s1t, s2t, stats)


def kernel(x, c, w_ada, b_ada, ada_layer, norm1_g, norm2_g, w_in, lam_re, lam_im, log_dt, b_re, b_im,
           c_re, c_im, d_skip, w_glu, q_gain, k_gain, sinks, gn_ssm, gn_attn, w_out, peer_wq, peer_k1,
           peer_k2, peer_u, peer_v):
    bsz, seq, d = x.shape
    t = bsz * seq
    depth = w_in.shape[0]
    n_mod = ada_layer.shape[1]
    ssm_w = w_glu.shape[1]
    attn_w = gn_attn.shape[1]
    kv_w = (w_in.shape[2] - ssm_w - attn_w) // 2

    cond = _cond(c, w_ada, b_ada).reshape(bsz, n_mod, d)
    xf = x.astype(F32).reshape(t, d)
    delta, gate_prev = None, None
    for l in range(depth):
        mod = cond + ada_layer[l].astype(F32)
        shift1, scale1, gate1, shift2, scale2, gate2 = (mod[:, i] for i in range(n_mod))

        xf, h = _adaln_norm(xf, delta, gate_prev, norm1_g[l], scale1, shift1, seq)
        w_in_l = w_in[l].astype(BF16)
        z_ssm = _matmul_channel_blocks(h, w_in_l[:, :ssm_w], BF16, tm=1024, tn=512)
        z_attn = _matmul(h, w_in_l[:, ssm_w:], BF16, tm=1024, tn=512)
        ops = _s5_operators(lam_re[l], lam_im[l], log_dt[l], b_re[l], b_im[l], c_re[l], c_im[l],
                            d_skip[l])
        g = _s5_core(z_ssm, ops, bsz)
        y_ssm = _glu_norm(g, w_glu[l].astype(BF16), gn_ssm[l])
        y_attn = _attention(z_attn, 0, attn_w, kv_w, q_gain[l], k_gain[l], sinks[l], gn_attn[l], bsz)
        mixed = _matmul_concat(y_ssm, y_attn, w_out[l].astype(BF16), F32, tm=1024, tn=512)

        xf, h2 = _adaln_norm(xf, mixed, gate1, norm2_g[l], scale2, shift2, seq)
        qt = _matmul_nt(peer_wq[l].T.astype(BF16), h2, F32, tn=512)
        s1t, s2t, stats = _peer_topk(qt, peer_k1[l], peer_k2[l])
        delta = _peer_dense(h2, peer_u[l].astype(BF16), peer_v[l].astype(BF16), s1t, s2t, stats)
        gate_prev = gate2
    out = _residual_add(xf, delta, gate_prev, seq)
    return out.reshape(bsz, seq, d).astype(x.dtype)
```

```python
import functools
import math

import jax
import jax.numpy as jnp
from jax import lax
from jax.experimental import pallas as pl
from jax.experimental.pallas import tpu as pltpu

F32 = jnp.float32
BF16 = jnp.bfloat16
EPS = 1e-6
WINDOW = 128
PEER_TOPK = 16
S5_CHUNK = 16
S5_GROUP_BLOCK = 8
V7X_VMEM_LIMIT = 56 * 1024 * 1024


def _params(semantics, vmem=V7X_VMEM_LIMIT, flags=None):
    return pltpu.CompilerParams(dimension_semantics=semantics, vmem_limit_bytes=vmem, flags=flags)


def _mm_kernel(a_ref, b_ref, o_ref):
    o_ref[...] = jnp.dot(a_ref[...], b_ref[...], preferred_element_type=F32).astype(o_ref.dtype)


def _matmul(a, b, layer, col0, n, out_dtype, tm, tn):
    m, k = a.shape
    j0 = col0 // tn
    return pl.pallas_call(
        _mm_kernel,
        grid=(m // tm, n // tn),
        in_specs=[pl.BlockSpec((tm, k), lambda i, j: (i, 0)),
                  pl.BlockSpec((None, k, tn), lambda i, j: (layer, 0, j0 + j))],
        out_specs=pl.BlockSpec((tm, tn), lambda i, j: (i, j)),
        out_shape=jax.ShapeDtypeStruct((m, n), out_dtype),
        compiler_params=_params(("parallel", "parallel")),
        name="matmul",
    )(a, b)


def _mm_cb_kernel(a_ref, b_ref, o_ref):
    res = jnp.dot(a_ref[...], b_ref[...], preferred_element_type=F32).astype(o_ref.dtype)
    lanes = o_ref.shape[2]
    for k in range(o_ref.shape[0]):
        o_ref[k] = res[:, k * lanes:(k + 1) * lanes]


def _matmul_channel_blocks(a, b, layer, n, out_dtype, tm, tn, lanes=128):
    m, k = a.shape
    return pl.pallas_call(
        _mm_cb_kernel,
        grid=(m // tm, n // tn),
        in_specs=[pl.BlockSpec((tm, k), lambda i, j: (i, 0)),
                  pl.BlockSpec((None, k, tn), lambda i, j: (layer, 0, j))],
        out_specs=pl.BlockSpec((tn // lanes, tm, lanes), lambda i, j: (j, i, 0)),
        out_shape=jax.ShapeDtypeStruct((n // lanes, m, lanes), out_dtype),
        compiler_params=_params(("parallel", "parallel")),
        name="matmul_channel_blocks",
    )(a, b)


def _mm2_kernel(a1_ref, a2_ref, b1_ref, b2_ref, o_ref):
    acc = jnp.dot(a1_ref[...], b1_ref[...], preferred_element_type=F32)
    acc += jnp.dot(a2_ref[...], b2_ref[...], preferred_element_type=F32)
    o_ref[...] = acc.astype(o_ref.dtype)


def _matmul_concat(a1, a2, b, layer, out_dtype, tm, tn):
    m, k1 = a1.shape
    k2 = a2.shape[1]
    assert k1 == k2
    n = b.shape[2]
    return pl.pallas_call(
        _mm2_kernel,
        grid=(m // tm, n // tn),
        in_specs=[pl.BlockSpec((tm, k1), lambda i, j: (i, 0)),
                  pl.BlockSpec((tm, k2), lambda i, j: (i, 0)),
                  pl.BlockSpec((None, k1, tn), lambda i, j: (layer, 0, j)),
                  pl.BlockSpec((None, k2, tn), lambda i, j: (layer, 1, j))],
        out_specs=pl.BlockSpec((tm, tn), lambda i, j: (i, j)),
        out_shape=jax.ShapeDtypeStruct((m, n), out_dtype),
        compiler_params=_params(("parallel", "parallel")),
        name="matmul_concat",
    )(a1, a2, b, b)


def _mm_nt_kernel(a_ref, b_ref, o_ref):
    o_ref[...] = lax.dot_general(a_ref[...], b_ref[...], (((1,), (1,)), ((), ())),
                                 preferred_element_type=F32).astype(o_ref.dtype)


def _matmul_nt(a, layer, b, out_dtype, tn):
    _, m, k = a.shape
    n = b.shape[0]
    return pl.pallas_call(
        _mm_nt_kernel,
        grid=(n // tn,),
        in_specs=[pl.BlockSpec((None, m, k), lambda j: (layer, 0, 0)),
                  pl.BlockSpec((tn, k), lambda j: (j, 0))],
        out_specs=pl.BlockSpec((m, tn), lambda j: (0, j)),
        out_shape=jax.ShapeDtypeStruct((m, n), out_dtype),
        compiler_params=_params(("parallel",)),
        name="matmul_nt",
    )(a, b)


def _cond_kernel(c_ref, w_ref, b_ref, o_ref):
    c = c_ref[...]
    s = c * jax.nn.sigmoid(c)
    o_ref[...] = jnp.dot(s, w_ref[...], preferred_element_type=F32,
                         precision=lax.Precision.HIGHEST) + b_ref[...]


def _cond(c, w_ada, b_ada, tn=512):
    bsz, d = c.shape
    n = w_ada.shape[1]
    rows = 8
    cp = jnp.zeros((rows, d), F32).at[:bsz].set(c.astype(F32))
    out = pl.pallas_call(
        _cond_kernel,
        grid=(n // tn,),
        in_specs=[pl.BlockSpec((rows, d), lambda j: (0, 0)),
                  pl.BlockSpec((d, tn), lambda j: (0, j)),
                  pl.BlockSpec((1, tn), lambda j: (0, j))],
        out_specs=pl.BlockSpec((rows, tn), lambda j: (0, j)),
        out_shape=jax.ShapeDtypeStruct((rows, n), F32),
        compiler_params=_params(("parallel",)),
        name="adaln_cond",
    )(cp, w_ada.astype(F32), b_ada.astype(F32).reshape(1, n))
    return out[:bsz]


def _norm_body(x, g_ref, scale_ref, shift_ref, h_ref):
    y = x * lax.rsqrt(jnp.mean(x * x, axis=-1, keepdims=True) + EPS)
    y = y * g_ref[...]
    h_ref[...] = (y * (1.0 + scale_ref[0]) + shift_ref[0]).astype(h_ref.dtype)


def _norm_kernel(x_ref, g_ref, scale_ref, shift_ref, h_ref):
    _norm_body(x_ref[...], g_ref, scale_ref, shift_ref, h_ref)


def _resnorm_kernel(x_ref, d_ref, gate_ref, g_ref, scale_ref, shift_ref, xo_ref, h_ref):
    x = x_ref[...] + gate_ref[0] * d_ref[...]
    xo_ref[...] = x
    _norm_body(x, g_ref, scale_ref, shift_ref, h_ref)


def _adaln_norm(x, delta, gate, g, scale, shift, rows_per_batch, tm=256):
    t, d = x.shape
    tpb = rows_per_batch // tm
    row = pl.BlockSpec((tm, d), lambda i: (i, 0))
    per_batch = pl.BlockSpec((1, 1, d), lambda i: (i // tpb, 0, 0))
    vec = pl.BlockSpec((1, d), lambda i: (0, 0))
    g2 = g.astype(F32).reshape(1, d)
    b3 = lambda v: v.astype(F32).reshape(v.shape[0], 1, d)
    if delta is None:
        h = pl.pallas_call(
            _norm_kernel,
            grid=(t // tm,),
            in_specs=[row, vec, per_batch, per_batch],
            out_specs=row,
            out_shape=jax.ShapeDtypeStruct((t, d), BF16),
            compiler_params=_params(("parallel",)),
            name="adaln_norm",
        )(x, g2, b3(scale), b3(shift))
        return x, h
    return pl.pallas_call(
        _resnorm_kernel,
        grid=(t // tm,),
        in_specs=[row, row, per_batch, vec, per_batch, per_batch],
        out_specs=[row, row],
        out_shape=[jax.ShapeDtypeStruct((t, d), F32), jax.ShapeDtypeStruct((t, d), BF16)],
        compiler_params=_params(("parallel",)),
        name="residual_adaln_norm",
    )(x, delta, b3(gate), g2, b3(scale), b3(shift))


def _residual_kernel(x_ref, d_ref, gate_ref, o_ref):
    o_ref[...] = x_ref[...] + gate_ref[0] * d_ref[...]


def _residual_add(x, delta, gate, rows_per_batch, tm=256):
    t, d = x.shape
    tpb = rows_per_batch // tm
    row = pl.BlockSpec((tm, d), lambda i: (i, 0))
    return pl.pallas_call(
        _residual_kernel,
        grid=(t // tm,),
        in_specs=[row, row, pl.BlockSpec((1, 1, d), lambda i: (i // tpb, 0, 0))],
        out_specs=row,
        out_shape=jax.ShapeDtypeStruct((t, d), F32),
        compiler_params=_params(("parallel",)),
        name="residual_add",
    )(x, delta, gate.astype(F32).reshape(gate.shape[0], 1, d))


def _s5_operators(lam_re, lam_im, log_dt, b_re, b_im, c_re, c_im, d_skip):
    hp = lax.Precision.HIGHEST
    j = S5_CHUNK
    f = lambda v: v.astype(F32)
    lr, li = f(lam_re), f(lam_im)
    g, p = lr.shape
    h = b_re.shape[-1]
    dt = jnp.exp(f(log_dt))[:, None]
    mag = jnp.exp(lr * dt)
    ar = mag * jnp.cos(li * dt)
    ai = mag * jnp.sin(li * dt)
    den = lr * lr + li * li
    kr = ((ar - 1.0) * lr + ai * li) / den
    ki = (ai * lr - (ar - 1.0) * li) / den
    br, bi = f(b_re), f(b_im)
    bbr = kr[..., None] * br - ki[..., None] * bi
    bbi = kr[..., None] * bi + ki[..., None] * br
    cr, ci = f(c_re), f(c_im)
    pr, pi = [jnp.ones_like(ar)], [jnp.zeros_like(ar)]
    for _ in range(j):
        pr.append(pr[-1] * ar - pi[-1] * ai)
        pi.append(pr[-2] * ai + pi[-1] * ar)
    pr, pi = jnp.stack(pr), jnp.stack(pi)
    er = pr[:j, :, :, None] * bbr - pi[:j, :, :, None] * bbi
    ei = pr[:j, :, :, None] * bbi + pi[:j, :, :, None] * bbr
    kk = (jnp.einsum('ghp,ngpk->nghk', cr, er, precision=hp)
          - jnp.einsum('ghp,ngpk->nghk', ci, ei, precision=hp))
    kk = kk.at[0].add(f(d_skip)[:, :, None] * jnp.eye(h, dtype=F32))
    gb = S5_GROUP_BLOCK
    nblk = g // gb
    t_op = kk.reshape(j, nblk, gb, h, h).transpose(1, 0, 4, 2, 3).reshape(nblk, j, h, gb * h)

    def chunk_in(e):
        return (e[::-1].reshape(j, nblk, gb, p, h).transpose(1, 0, 4, 2, 3)
                .reshape(nblk, j * h, gb * p))

    p1r, p1i = pr[1:], pi[1:]
    cc_r = cr[None] * p1r[:, :, None, :] - ci[None] * p1i[:, :, None, :]
    cc_i = -(cr[None] * p1i[:, :, None, :] + ci[None] * p1r[:, :, None, :])

    def chunk_out(m):
        return (m.reshape(j, nblk, gb, h, p).transpose(1, 4, 0, 2, 3)
                .reshape(nblk, p, j * gb * h))

    a16r = pr[j].reshape(nblk, 1, gb * p)
    a16i = pi[j].reshape(nblk, 1, gb * p)
    return (t_op.astype(BF16), chunk_in(er).astype(BF16), chunk_in(ei).astype(BF16),
            chunk_out(cc_r).astype(BF16), chunk_out(cc_i).astype(BF16), a16r, a16i)


def _s5_kernel(u_ref, t_ref, bre_ref, bim_ref, cre_ref, cim_ref, ar_ref, ai_ref, mt_ref, mb_ref,
               o_ref, tblk, bblk_r, bblk_i, cblk_r, cblk_i, zre, zim, pre, pim,
               *, chunks_per_batch, n_batch):
    gb = S5_GROUP_BLOCK
    j = S5_CHUNK
    hh = t_ref.shape[2]
    pp = cre_ref.shape[1]
    lanes = gb * hh
    ncol = 4 * lanes
    for i in range(j):
        j_first = (i * lanes // ncol) * ncol // lanes
        for gl in range(gb):
            src = slice(i * hh, (i + 1) * hh)
            dst = slice(i * lanes + gl * hh, i * lanes + (gl + 1) * hh)
            m_gl = mt_ref[gl, :, 0:lanes]
            for jj in range(j_first, j):
                blk = t_ref[0, jj - i] * m_gl if jj >= i else jnp.zeros((hh, lanes), tblk.dtype)
                tblk[dst, jj * lanes:(jj + 1) * lanes] = blk
            bblk_r[dst, :] = bre_ref[0, src, :] * mb_ref[gl]
            bblk_i[dst, :] = bim_ref[0, src, :] * mb_ref[gl]
    for gl in range(gb):
        for r0 in range(0, pp, hh):
            src = slice(r0, r0 + hh)
            dst = slice(gl * pp + r0, gl * pp + r0 + hh)
            cblk_r[dst, :] = cre_ref[0, src, :] * mt_ref[gl]
            cblk_i[dst, :] = cim_ref[0, src, :] * mt_ref[gl]

    u = u_ref[0]
    zre[...] = jnp.dot(u, bblk_r[...], preferred_element_type=F32)
    zim[...] = jnp.dot(u, bblk_i[...], preferred_element_type=F32)
    ar = ar_ref[0]
    ai = ai_ref[0]

    def step(c, carry):
        new = []
        for b in range(n_batch):
            sr, si = carry[2 * b], carry[2 * b + 1]
            row = pl.ds(b * chunks_per_batch + c, 1)
            pre[row, :] = sr
            pim[row, :] = si
            new.append(ar * sr - ai * si + zre[row, :])
            new.append(ar * si + ai * sr + zim[row, :])
        return tuple(new)

    zero = jnp.zeros((1, zre.shape[1]), F32)
    lax.fori_loop(0, chunks_per_batch, step, (zero,) * (2 * n_batch), unroll=8)

    pr_b = pre[...].astype(BF16)
    pi_b = pim[...].astype(BF16)
    for c0 in range(0, j * lanes, ncol):
        cols = slice(c0, c0 + ncol)
        k_rows = c0 + ncol
        y = jnp.dot(u[:, :k_rows], tblk[:k_rows, cols], preferred_element_type=F32)
        y += jnp.dot(pr_b, cblk_r[:, cols], preferred_element_type=F32)
        y += jnp.dot(pi_b, cblk_i[:, cols], preferred_element_type=F32)
        o_ref[0, :, cols] = jax.nn.gelu(y).astype(o_ref.dtype)


def _s5_core(ub, ops, n_batch):
    t_op, bre, bim, cre, cim, a16r, a16i = ops
    nblk, t, lanes = ub.shape
    j = S5_CHUNK
    gb = S5_GROUP_BLOCK
    hh = lanes // gb
    nchunk = t // j
    pp = cre.shape[1]
    uc = ub.reshape(nblk, nchunk, j * lanes)
    def own(width, per_group):
        col_group = jnp.arange(width) // per_group % gb
        m = col_group[None, None, :] == jnp.arange(gb)[:, None, None]
        return jnp.broadcast_to(m, (gb, hh, width)).astype(BF16)

    mask_t = own(j * lanes, hh)
    mask_b = own(gb * pp, pp)
    blk = lambda r, c: pl.BlockSpec((1, r, c), lambda i: (i, 0, 0))
    const = lambda a: pl.BlockSpec(a.shape, lambda i: (0, 0, 0))
    yc = pl.pallas_call(
        functools.partial(_s5_kernel, chunks_per_batch=nchunk // n_batch, n_batch=n_batch),
        grid=(nblk,),
        in_specs=[blk(nchunk, j * lanes), pl.BlockSpec((1, j, hh, lanes), lambda i: (i, 0, 0, 0)),
                  blk(j * hh, gb * pp), blk(j * hh, gb * pp),
                  blk(pp, j * lanes), blk(pp, j * lanes),
                  blk(1, gb * pp), blk(1, gb * pp), const(mask_t), const(mask_b)],
        out_specs=blk(nchunk, j * lanes),
        out_shape=jax.ShapeDtypeStruct((nblk, nchunk, j * lanes), BF16),
        scratch_shapes=[pltpu.VMEM((j * lanes, j * lanes), BF16),
                        pltpu.VMEM((j * lanes, gb * pp), BF16), pltpu.VMEM((j * lanes, gb * pp), BF16),
                        pltpu.VMEM((gb * pp, j * lanes), BF16), pltpu.VMEM((gb * pp, j * lanes), BF16)]
                       + [pltpu.VMEM((nchunk, gb * pp), F32)] * 4,
        compiler_params=_params(("parallel",)),
        name="s5_chunked_scan",
    )(uc, t_op, bre, bim, cre, cim, a16r, a16i, mask_t, mask_b)
    return yc.reshape(nblk, t, lanes)


def _glu_norm_kernel(g_ref, w_ref, gn_ref, o_ref):
    g = jnp.concatenate([g_ref[i] for i in range(g_ref.shape[0])], axis=1)
    gate = jax.nn.sigmoid(jnp.dot(g, w_ref[...], preferred_element_type=F32))
    y = g.astype(F32) * gate
    y = y * lax.rsqrt(jnp.mean(y * y, axis=-1, keepdims=True) + EPS)
    o_ref[...] = (y * gn_ref[...]).astype(o_ref.dtype)


def _glu_norm(gb_, w_glu, layer, gn, tm=512):
    nblk, t, lanes = gb_.shape
    w = nblk * lanes
    return pl.pallas_call(
        _glu_norm_kernel,
        grid=(t // tm,),
        in_specs=[pl.BlockSpec((nblk, tm, lanes), lambda i: (0, i, 0)),
                  pl.BlockSpec((None, w, w), lambda i: (layer, 0, 0)),
                  pl.BlockSpec((1, w), lambda i: (0, 0))],
        out_specs=pl.BlockSpec((tm, w), lambda i: (i, 0)),
        out_shape=jax.ShapeDtypeStruct((t, w), BF16),
        compiler_params=_params(("parallel",)),
        name="glu_groupnorm",
    )(gb_, w_glu, gn.astype(F32).reshape(1, w))


def _head_rms_scale(x, sel_ref, exp_ref, head_dim):
    ss = jnp.dot((x * x).astype(BF16), sel_ref[...], preferred_element_type=F32)
    r = lax.rsqrt(ss * (1.0 / head_dim) + EPS)
    r_hi = r.astype(BF16)
    r_lo = (r - r_hi.astype(F32)).astype(BF16)
    return (jnp.dot(r_hi, exp_ref[...], preferred_element_type=F32)
            + jnp.dot(r_lo, exp_ref[...], preferred_element_type=F32))


def _attn_kernel(sink_ref, q_ref, kp_ref, kc_ref, vp_ref, vc_ref, qg_ref, kg_ref, gn_ref,
                 qsel_ref, qexp_ref, ksel_ref, kexp_ref, o_ref, acc,
                 *, n_q_heads, n_kv_heads, head_dim):
    nblk = pl.program_id(1)
    w = WINDOW
    lw = 2 * head_dim
    grp = n_q_heads // n_kv_heads
    t_loc = lax.broadcasted_iota(jnp.int32, (w, 2 * w), 0)
    s_loc = lax.broadcasted_iota(jnp.int32, (w, 2 * w), 1)
    dist = t_loc + w - s_loc
    valid = (dist >= 0) & (dist < w) & (s_loc + nblk * w >= w)
    dmask = jnp.where(valid, dist.astype(F32), jnp.inf)

    q = q_ref[...].astype(F32)
    qn = (q * _head_rms_scale(q, qsel_ref, qexp_ref, head_dim) * qg_ref[...]).astype(BF16)
    k = jnp.concatenate([kp_ref[...], kc_ref[...]], axis=0).astype(F32)
    kn = k * _head_rms_scale(k, ksel_ref, kexp_ref, head_dim) * kg_ref[...]
    v = jnp.concatenate([vp_ref[...], vc_ref[...]], axis=0).astype(F32)

    lane = lax.broadcasted_iota(jnp.int32, (2 * w, lw), 1)
    lo = lane < head_dim

    def block_diag(x, hk):
        grp_lanes = x[:, (hk // 2) * lw:(hk // 2 + 1) * lw]
        swapped = pltpu.roll(grp_lanes, head_dim, 1)
        low, high = (grp_lanes, swapped) if hk % 2 == 0 else (swapped, grp_lanes)
        return jnp.concatenate([jnp.where(lo, low, 0.0), jnp.where(lo, 0.0, high)],
                               axis=0).astype(BF16)

    lane_q = lax.broadcasted_iota(jnp.int32, (w, lw), 1) < head_dim
    for hk in range(n_kv_heads):
        kb = block_diag(kn, hk)
        vb = block_diag(v, hk)
        for pq in range(grp // 2):
            pair = hk * (grp // 2) + pq
            cols = slice(pair * lw, (pair + 1) * lw)
            s = lax.dot_general(qn[:, cols], kb, (((1,), (1,)), ((), ())),
                                preferred_element_type=F32)
            es, inv = [], []
            for i in range(2):
                hq = 2 * pair + i
                slope = 2.0 ** (-8.0 * (hq + 1) / n_q_heads)
                si = s[:, i * 2 * w:(i + 1) * 2 * w] - slope * dmask
                sink = sink_ref[hq]
                m = jnp.maximum(jnp.max(si, axis=-1, keepdims=True), sink)
                e = jnp.exp(si - m)
                inv.append(1.0 / (jnp.sum(e, axis=-1, keepdims=True) + jnp.exp(sink - m)))
                es.append(e.astype(BF16))
            pv = jnp.dot(jnp.concatenate(es, axis=1), vb, preferred_element_type=F32)
            acc[:, cols] = pv * jnp.where(lane_q, inv[0], inv[1])
    y = acc[...]
    y = y * lax.rsqrt(jnp.mean(y * y, axis=-1, keepdims=True) + EPS)
    o_ref[...] = (y * gn_ref[...]).astype(o_ref.dtype)


def _attention(z, ssm_w, attn_w, kv_w, q_gain, k_gain, sinks, gn, n_batch):
    t = z.shape[0]
    w = WINDOW
    head_dim = q_gain.shape[0]
    n_q = attn_w // head_dim
    n_kv = kv_w // head_dim
    nb = t // n_batch // w
    qcol = ssm_w // attn_w
    kcol = (ssm_w + attn_w) // kv_w
    vcol = kcol + 1
    cur = lambda col: (lambda b, n: (b * nb + n, col))
    prev = lambda col: (lambda b, n: (b * nb + jnp.maximum(n - 1, 0), col))
    full = lambda r, c: pl.BlockSpec((r, c), lambda b, n: (0, 0))
    nsel = 128

    def selectors(width):
        sel = (jnp.arange(width)[:, None] // head_dim == jnp.arange(nsel)[None, :]).astype(BF16)
        return sel, sel.T

    qsel, qexp = selectors(attn_w)
    ksel, kexp = selectors(kv_w)
    qg_row = (jnp.tile(q_gain.astype(F32), n_q) * head_dim ** -0.5).reshape(1, attn_w)
    kg_row = jnp.tile(k_gain.astype(F32), n_kv).reshape(1, kv_w)
    return pl.pallas_call(
        functools.partial(_attn_kernel, n_q_heads=n_q, n_kv_heads=n_kv, head_dim=head_dim),
        grid=(n_batch, nb),
        in_specs=[pl.BlockSpec(memory_space=pltpu.SMEM),
                  pl.BlockSpec((w, attn_w), cur(qcol)),
                  pl.BlockSpec((w, kv_w), prev(kcol)),
                  pl.BlockSpec((w, kv_w), cur(kcol)),
                  pl.BlockSpec((w, kv_w), prev(vcol)),
                  pl.BlockSpec((w, kv_w), cur(vcol)),
                  full(1, attn_w), full(1, kv_w), full(1, attn_w),
                  full(attn_w, nsel), full(nsel, attn_w), full(kv_w, nsel), full(nsel, kv_w)],
        out_specs=pl.BlockSpec((w, attn_w), lambda b, n: (b * nb + n, 0)),
        out_shape=jax.ShapeDtypeStruct((t, attn_w), BF16),
        scratch_shapes=[pltpu.VMEM((w, attn_w), F32)],
        compiler_params=_params(("parallel", "parallel")),
        name="swa_attention",
    )(sinks.astype(F32), z, z, z, z, z, qg_row, kg_row, gn.astype(F32).reshape(1, attn_w),
      qsel, qexp, ksel, kexp)


def _kth_largest_rows(vals, k):
    out = []
    work = vals
    for _ in range(k):
        m = jnp.max(work, axis=0, keepdims=True)
        out.append(m)
        work = jnp.where(work == m, -jnp.inf, work)
    return out


def _kth_largest_value(vals, k):
    work = vals
    left = jnp.full((1, vals.shape[1]), float(k), F32)
    kth = jnp.full((1, vals.shape[1]), -jnp.inf, F32)
    for _ in range(k):
        m = jnp.max(work, axis=0, keepdims=True)
        hit = work == m
        kth = jnp.where(left > 0.0, m, kth)
        left = left - jnp.sum(jnp.where(hit, 1.0, 0.0), axis=0, keepdims=True)
        work = jnp.where(hit, -jnp.inf, work)
    return kth


def _staircase_candidates(v1, v2):
    k = PEER_TOPK
    sub = 8
    v1m = jnp.concatenate(v1, axis=0)
    v2m = jnp.concatenate(v2, axis=0)
    row = lax.broadcasted_iota(jnp.int32, (sub, v2m.shape[1]), 0)
    blocks = [v1[0] + v2m]
    for i in range(1, sub):
        keep = k // (i + 1)
        blk = v1[i] + v2m[0:sub]
        blocks.append(blk if keep >= sub else jnp.where(row < keep, blk, -jnp.inf))
    blocks.append(v1m[sub:] + v2[0])
    return jnp.concatenate(blocks, axis=0)


def _peer_topk_kernel(qt_ref, k1_ref, k2_ref, s1_ref, s2_ref, st_ref, *, n_heads, n_keys):
    half = k1_ref.shape[1]
    hp = lax.Precision.HIGHEST
    k1 = k1_ref[...]
    k2 = k2_ref[...]
    taus, c1s, m2s = [], [], []
    for h in range(n_heads):
        base = h * 2 * half
        s1 = jnp.dot(k1, qt_ref[base:base + half, :], preferred_element_type=F32, precision=hp)
        s2 = jnp.dot(k2, qt_ref[base + half:base + 2 * half, :], preferred_element_type=F32,
                     precision=hp)
        s1_ref[h * n_keys:(h + 1) * n_keys, :] = s1
        s2_ref[h * n_keys:(h + 1) * n_keys, :] = s2
        v1 = _kth_largest_rows(s1, PEER_TOPK)
        v2 = _kth_largest_rows(s2, PEER_TOPK)
        cand = _staircase_candidates(v1, v2)
        tau = _kth_largest_value(cand, PEER_TOPK)
        top = v1[0] + v2[0]
        zsum = jnp.sum(jnp.where(cand >= tau, jnp.exp(cand - top), 0.0), axis=0, keepdims=True)
        taus.append(tau)
        c1s.append(v1[0] + jnp.log(zsum))
        m2s.append(v2[0])
    pad = jnp.zeros((st_ref.shape[0] - 3 * n_heads, st_ref.shape[1]), F32)
    st_ref[...] = jnp.concatenate(taus + c1s + m2s + [pad], axis=0)


def _peer_topk(qt, k1, k2, tn=256):
    hq, t = qt.shape
    n_keys, half = k1.shape
    n_heads = hq // (2 * half)
    srows = n_heads * n_keys
    col = lambda rows: pl.BlockSpec((rows, tn), lambda i: (0, i))
    return pl.pallas_call(
        functools.partial(_peer_topk_kernel, n_heads=n_heads, n_keys=n_keys),
        grid=(t // tn,),
        in_specs=[col(hq),
                  pl.BlockSpec((n_keys, half), lambda i: (0, 0)),
                  pl.BlockSpec((n_keys, half), lambda i: (0, 0))],
        out_specs=[col(srows), col(srows), col(32)],
        out_shape=[jax.ShapeDtypeStruct((srows, t), F32), jax.ShapeDtypeStruct((srows, t), F32),
                   jax.ShapeDtypeStruct((32, t), F32)],
        compiler_params=_params(("parallel",)),
        name="peer_topk",
    )(qt, k1.astype(F32), k2.astype(F32))


def _peer_dense_kernel(h_ref, u_ref, v_ref, s1_ref, s2_ref, st_ref, o_ref, e2, tau8, s1b, e1b, wt,
                       *, n_heads, n_keys):
    j = pl.program_id(1)
    te = u_ref.shape[0]
    tm = h_ref.shape[0]
    d = v_ref.shape[1]
    na = te // n_keys
    sub = 8
    dc = 1024

    @pl.when(j == 0)
    def _():
        o_ref[...] = jnp.zeros_like(o_ref)
        for h in range(n_heads):
            rows = slice(h * n_keys, (h + 1) * n_keys)
            e2[rows, :] = jnp.exp(s2_ref[rows, :] - st_ref[2 * n_heads + h:2 * n_heads + h + 1, :])
            tau8[h * sub:(h + 1) * sub, :] = jnp.broadcast_to(st_ref[h:h + 1, :], (sub, tm))

    for aa in range(na):
        for h in range(n_heads):
            row = s1_ref[pl.ds(h * n_keys + j * na + aa, 1), :]
            k = (aa * n_heads + h) * sub
            s1b[k:k + sub, :] = jnp.broadcast_to(row, (sub, tm))
            e1b[k:k + sub, :] = jnp.broadcast_to(
                jnp.exp(row - st_ref[n_heads + h:n_heads + h + 1, :]), (sub, tm))

    ax = 2
    link = jnp.zeros((sub, tm), F32)
    for a0 in range(0, na, ax):
        for b0 in range(0, n_keys, 2 * sub):
            acc = [[link, link] for _ in range(ax)]
            for h in range(n_heads):
                t8 = tau8[h * sub:(h + 1) * sub, :]
                r2 = [slice(h * n_keys + b0 + y * sub, h * n_keys + b0 + (y + 1) * sub) for y in range(2)]
                s2v = [s2_ref[r, :] for r in r2]
                e2v = [e2[r, :] for r in r2]
                for x in range(ax):
                    k = ((a0 + x) * n_heads + h) * sub
                    s1v = s1b[k:k + sub, :]
                    e1v = e1b[k:k + sub, :]
                    for y in range(2):
                        acc[x][y] = acc[x][y] + jnp.where(s1v + s2v[y] >= t8, e1v * e2v[y], 0.0)
            for x in range(ax):
                r0 = (a0 + x) * n_keys + b0
                wt[r0:r0 + 2 * sub, :] = jnp.concatenate(acc[x], axis=0).astype(wt.dtype)
            link = jnp.minimum(pltpu.roll(acc[0][0], 1, 1), 0.0)

    w = wt[...].T
    act = lax.dot_general(h_ref[...], u_ref[...], (((1,), (1,)), ((), ())),
                          preferred_element_type=F32)
    g = jax.nn.gelu(act).astype(BF16) * w
    for c0 in range(0, d, dc):
        o_ref[:, c0:c0 + dc] += jnp.dot(g, v_ref[:, c0:c0 + dc], preferred_element_type=F32)


def _peer_dense(h2, u_tab, v_tab, layer, s1t, s2t, stats, tm=512, te=512):
    t, d = h2.shape
    ne = u_tab.shape[1]
    n_keys = int(round(math.sqrt(ne)))
    n_heads = s1t.shape[0] // n_keys
    srows = s1t.shape[0]
    na = te // n_keys
    once = pl.Buffered(1)
    tok = lambda rows: pl.BlockSpec((rows, tm), lambda i, j: (0, i), pipeline_mode=once)
    return pl.pallas_call(
        functools.partial(_peer_dense_kernel, n_heads=n_heads, n_keys=n_keys),
        grid=(t // tm, ne // te),
        in_specs=[pl.BlockSpec((tm, d), lambda i, j: (i, 0), pipeline_mode=once),
                  pl.BlockSpec((None, te, d), lambda i, j: (layer, j, 0)),
                  pl.BlockSpec((None, te, d), lambda i, j: (layer, j, 0)),
                  tok(srows), tok(srows), tok(32)],
        out_specs=pl.BlockSpec((tm, d), lambda i, j: (i, 0)),
        out_shape=jax.ShapeDtypeStruct((t, d), F32),
        scratch_shapes=[pltpu.VMEM((srows, tm), F32),
                        pltpu.VMEM((n_heads * 8, tm), F32),
                        pltpu.VMEM((na * n_heads * 8, tm), F32),
                        pltpu.VMEM((na * n_heads * 8, tm), F32),
                        pltpu.VMEM((te, tm), BF16)],
        compiler_params=_params(("parallel", "arbitrary")),
        name="peer_dense",
    )(h2, u_tab, v_tab, s1t, s2t, stats)


def kernel(x, c, w_ada, b_ada, ada_layer, norm1_g, norm2_g, w_in, lam_re, lam_im, log_dt, b_re, b_im,
           c_re, c_im, d_skip, w_glu, q_gain, k_gain, sinks, gn_ssm, gn_attn, w_out, peer_wq, peer_k1,
           peer_k2, peer_u, peer_v):
    bsz, seq, d = x.shape
    t = bsz * seq
    depth = w_in.shape[0]
    n_mod = ada_layer.shape[1]
    ssm_w = w_glu.shape[1]
    attn_w = gn_attn.shape[1]
    kv_w = (w_in.shape[2] - ssm_w - attn_w) // 2

    w_in_b, w_glu_b, w_out_b = w_in.astype(BF16), w_glu.astype(BF16), w_out.astype(BF16)
    wq_t_b = peer_wq.transpose(0, 2, 1).astype(BF16)
    u_b, v_b = peer_u.astype(BF16), peer_v.astype(BF16)

    cond = _cond(c, w_ada, b_ada).reshape(bsz, n_mod, d)
    xf = x.astype(F32).reshape(t, d)
    delta, gate_prev = None, None
    for l in range(depth):
        mod = cond + ada_layer[l].astype(F32)
        shift1, scale1, gate1, shift2, scale2, gate2 = (mod[:, i] for i in range(n_mod))

        xf, h = _adaln_norm(xf, delta, gate_prev, norm1_g[l], scale1, shift1, seq)
        z_ssm = _matmul_channel_blocks(h, w_in_b, l, ssm_w, BF16, tm=1024, tn=512)
        z_attn = _matmul(h, w_in_b, l, ssm_w, attn_w + 2 * kv_w, BF16, tm=1024, tn=512)
        ops = _s5_operators(lam_re[l], lam_im[l], log_dt[l], b_re[l], b_im[l], c_re[l], c_im[l],
                            d_skip[l])
        g = _s5_core(z_ssm, ops, bsz)
        y_ssm = _glu_norm(g, w_glu_b, l, gn_ssm[l])
        y_attn = _attention(z_attn, 0, attn_w, kv_w, q_gain[l], k_gain[l], sinks[l], gn_attn[l], bsz)
        mixed = _matmul_concat(y_ssm, y_attn, w_out_b, l, F32, tm=1024, tn=512)

        xf, h2 = _adaln_norm(xf, mixed, gate1, norm2_g[l], scale2, shift2, seq)
        qt = _matmul_nt(wq_t_b, l, h2, F32, tn=512)
        s1t, s2t, stats = _peer_topk(qt, peer_k1[l], peer_k2[l])
        delta = _peer_dense(h2, u_b, v_b, l, s1t, s2t, stats)
        gate_prev = gate2
    out = _residual_add(xf, delta, gate_prev, seq)
    return out.reshape(bsz, seq, d).astype(x.dtype)
```

```python
import functools
import math

import jax
import jax.numpy as jnp
from jax import lax
from jax.experimental import pallas as pl
from jax.experimental.pallas import tpu as pltpu

F32 = jnp.float32
BF16 = jnp.bfloat16
EPS = 1e-6
WINDOW = 128
PEER_TOPK = 16
LOG2E = 1.4426950408889634
S5_CHUNK = 16
S5_GROUP_BLOCK = 8
V7X_VMEM_LIMIT = 56 * 1024 * 1024


def _params(semantics, vmem=V7X_VMEM_LIMIT, flags=None):
    return pltpu.CompilerParams(dimension_semantics=semantics, vmem_limit_bytes=vmem, flags=flags)


def _mm_kernel(a_ref, b_ref, o_ref):
    o_ref[...] = jnp.dot(a_ref[...], b_ref[...], preferred_element_type=F32).astype(o_ref.dtype)


def _matmul(a, b, layer, col0, n, out_dtype, tm, tn):
    m, k = a.shape
    j0 = col0 // tn
    return pl.pallas_call(
        _mm_kernel,
        grid=(m // tm, n // tn),
        in_specs=[pl.BlockSpec((tm, k), lambda i, j: (i, 0)),
                  pl.BlockSpec((None, k, tn), lambda i, j: (layer, 0, j0 + j))],
        out_specs=pl.BlockSpec((tm, tn), lambda i, j: (i, j)),
        out_shape=jax.ShapeDtypeStruct((m, n), out_dtype),
        compiler_params=_params(("parallel", "parallel")),
        name="matmul",
    )(a, b)


def _mm_cb_kernel(a_ref, b_ref, o_ref):
    res = jnp.dot(a_ref[...], b_ref[...], preferred_element_type=F32).astype(o_ref.dtype)
    lanes = o_ref.shape[2]
    for k in range(o_ref.shape[0]):
        o_ref[k] = res[:, k * lanes:(k + 1) * lanes]


def _matmul_channel_blocks(a, b, layer, n, out_dtype, tm, tn, lanes=128):
    m, k = a.shape
    return pl.pallas_call(
        _mm_cb_kernel,
        grid=(m // tm, n // tn),
        in_specs=[pl.BlockSpec((tm, k), lambda i, j: (i, 0)),
                  pl.BlockSpec((None, k, tn), lambda i, j: (layer, 0, j))],
        out_specs=pl.BlockSpec((tn // lanes, tm, lanes), lambda i, j: (j, i, 0)),
        out_shape=jax.ShapeDtypeStruct((n // lanes, m, lanes), out_dtype),
        compiler_params=_params(("parallel", "parallel")),
        name="matmul_channel_blocks",
    )(a, b)


def _mm2_kernel(a1_ref, a2_ref, b1_ref, b2_ref, o_ref):
    acc = jnp.dot(a1_ref[...], b1_ref[...], preferred_element_type=F32)
    acc += jnp.dot(a2_ref[...], b2_ref[...], preferred_element_type=F32)
    o_ref[...] = acc.astype(o_ref.dtype)


def _matmul_concat(a1, a2, b, layer, out_dtype, tm, tn):
    m, k1 = a1.shape
    k2 = a2.shape[1]
    assert k1 == k2
    n = b.shape[2]
    return pl.pallas_call(
        _mm2_kernel,
        grid=(m // tm, n // tn),
        in_specs=[pl.BlockSpec((tm, k1), lambda i, j: (i, 0)),
                  pl.BlockSpec((tm, k2), lambda i, j: (i, 0)),
                  pl.BlockSpec((None, k1, tn), lambda i, j: (layer, 0, j)),
                  pl.BlockSpec((None, k2, tn), lambda i, j: (layer, 1, j))],
        out_specs=pl.BlockSpec((tm, tn), lambda i, j: (i, j)),
        out_shape=jax.ShapeDtypeStruct((m, n), out_dtype),
        compiler_params=_params(("parallel", "parallel")),
        name="matmul_concat",
    )(a1, a2, b, b)


def _mm_nt_kernel(a_ref, b_ref, o_ref):
    o_ref[...] = lax.dot_general(a_ref[...], b_ref[...], (((1,), (1,)), ((), ())),
                                 preferred_element_type=F32).astype(o_ref.dtype)


def _matmul_nt(a, layer, b, out_dtype, tn):
    _, m, k = a.shape
    n = b.shape[0]
    return pl.pallas_call(
        _mm_nt_kernel,
        grid=(n // tn,),
        in_specs=[pl.BlockSpec((None, m, k), lambda j: (layer, 0, 0)),
                  pl.BlockSpec((tn, k), lambda j: (j, 0))],
        out_specs=pl.BlockSpec((m, tn), lambda j: (0, j)),
        out_shape=jax.ShapeDtypeStruct((m, n), out_dtype),
        compiler_params=_params(("parallel",)),
        name="matmul_nt",
    )(a, b)


def _cond_kernel(c_ref, w_ref, b_ref, o_ref):
    c = c_ref[...]
    s = c * jax.nn.sigmoid(c)
    o_ref[...] = jnp.dot(s, w_ref[...], preferred_element_type=F32,
                         precision=lax.Precision.HIGHEST) + b_ref[...]


def _cond(c, w_ada, b_ada, tn=512):
    bsz, d = c.shape
    n = w_ada.shape[1]
    rows = 8
    cp = jnp.zeros((rows, d), F32).at[:bsz].set(c.astype(F32))
    out = pl.pallas_call(
        _cond_kernel,
        grid=(n // tn,),
        in_specs=[pl.BlockSpec((rows, d), lambda j: (0, 0)),
                  pl.BlockSpec((d, tn), lambda j: (0, j)),
                  pl.BlockSpec((1, tn), lambda j: (0, j))],
        out_specs=pl.BlockSpec((rows, tn), lambda j: (0, j)),
        out_shape=jax.ShapeDtypeStruct((rows, n), F32),
        compiler_params=_params(("parallel",)),
        name="adaln_cond",
    )(cp, w_ada.astype(F32), b_ada.astype(F32).reshape(1, n))
    return out[:bsz]


def _norm_body(x, g_ref, scale_ref, shift_ref, h_ref):
    y = x * lax.rsqrt(jnp.mean(x * x, axis=-1, keepdims=True) + EPS)
    y = y * g_ref[...]
    h_ref[...] = (y * (1.0 + scale_ref[0]) + shift_ref[0]).astype(h_ref.dtype)


def _norm_kernel(x_ref, g_ref, scale_ref, shift_ref, h_ref):
    _norm_body(x_ref[...], g_ref, scale_ref, shift_ref, h_ref)


def _resnorm_kernel(x_ref, d_ref, gate_ref, g_ref, scale_ref, shift_ref, xo_ref, h_ref):
    x = x_ref[...] + gate_ref[0] * d_ref[...]
    xo_ref[...] = x
    _norm_body(x, g_ref, scale_ref, shift_ref, h_ref)


def _adaln_norm(x, delta, gate, g, scale, shift, rows_per_batch, tm=256):
    t, d = x.shape
    tpb = rows_per_batch // tm
    row = pl.BlockSpec((tm, d), lambda i: (i, 0))
    per_batch = pl.BlockSpec((1, 1, d), lambda i: (i // tpb, 0, 0))
    vec = pl.BlockSpec((1, d), lambda i: (0, 0))
    g2 = g.astype(F32).reshape(1, d)
    b3 = lambda v: v.astype(F32).reshape(v.shape[0], 1, d)
    if delta is None:
        h = pl.pallas_call(
            _norm_kernel,
            grid=(t // tm,),
            in_specs=[row, vec, per_batch, per_batch],
            out_specs=row,
            out_shape=jax.ShapeDtypeStruct((t, d), BF16),
            compiler_params=_params(("parallel",)),
            name="adaln_norm",
        )(x, g2, b3(scale), b3(shift))
        return x, h
    return pl.pallas_call(
        _resnorm_kernel,
        grid=(t // tm,),
        in_specs=[row, row, per_batch, vec, per_batch, per_batch],
        out_specs=[row, row],
        out_shape=[jax.ShapeDtypeStruct((t, d), F32), jax.ShapeDtypeStruct((t, d), BF16)],
        compiler_params=_params(("parallel",)),
        name="residual_adaln_norm",
    )(x, delta, b3(gate), g2, b3(scale), b3(shift))


def _residual_kernel(x_ref, d_ref, gate_ref, o_ref):
    o_ref[...] = x_ref[...] + gate_ref[0] * d_ref[...]


def _residual_add(x, delta, gate, rows_per_batch, tm=256):
    t, d = x.shape
    tpb = rows_per_batch // tm
    row = pl.BlockSpec((tm, d), lambda i: (i, 0))
    return pl.pallas_call(
        _residual_kernel,
        grid=(t // tm,),
        in_specs=[row, row, pl.BlockSpec((1, 1, d), lambda i: (i // tpb, 0, 0))],
        out_specs=row,
        out_shape=jax.ShapeDtypeStruct((t, d), F32),
        compiler_params=_params(("parallel",)),
        name="residual_add",
    )(x, delta, gate.astype(F32).reshape(gate.shape[0], 1, d))


def _s5_operators(lam_re, lam_im, log_dt, b_re, b_im, c_re, c_im, d_skip):
    hp = lax.Precision.HIGHEST
    j = S5_CHUNK
    f = lambda v: v.astype(F32)
    lr, li = f(lam_re), f(lam_im)
    g, p = lr.shape
    h = b_re.shape[-1]
    dt = jnp.exp(f(log_dt))[:, None]
    mag = jnp.exp(lr * dt)
    ar = mag * jnp.cos(li * dt)
    ai = mag * jnp.sin(li * dt)
    den = lr * lr + li * li
    kr = ((ar - 1.0) * lr + ai * li) / den
    ki = (ai * lr - (ar - 1.0) * li) / den
    br, bi = f(b_re), f(b_im)
    bbr = kr[..., None] * br - ki[..., None] * bi
    bbi = kr[..., None] * bi + ki[..., None] * br
    cr, ci = f(c_re), f(c_im)
    pr, pi = [jnp.ones_like(ar)], [jnp.zeros_like(ar)]
    for _ in range(j):
        pr.append(pr[-1] * ar - pi[-1] * ai)
        pi.append(pr[-2] * ai + pi[-1] * ar)
    pr, pi = jnp.stack(pr), jnp.stack(pi)
    er = pr[:j, :, :, None] * bbr - pi[:j, :, :, None] * bbi
    ei = pr[:j, :, :, None] * bbi + pi[:j, :, :, None] * bbr
    kk = (jnp.einsum('ghp,ngpk->nghk', cr, er, precision=hp)
          - jnp.einsum('ghp,ngpk->nghk', ci, ei, precision=hp))
    kk = kk.at[0].add(f(d_skip)[:, :, None] * jnp.eye(h, dtype=F32))
    gb = S5_GROUP_BLOCK
    nblk = g // gb
    t_op = kk.reshape(j, nblk, gb, h, h).transpose(1, 0, 4, 2, 3).reshape(nblk, j, h, gb * h)

    def chunk_in(e):
        return (e[::-1].reshape(j, nblk, gb, p, h).transpose(1, 0, 4, 2, 3)
                .reshape(nblk, j * h, gb * p))

    p1r, p1i = pr[1:], pi[1:]
    cc_r = cr[None] * p1r[:, :, None, :] - ci[None] * p1i[:, :, None, :]
    cc_i = -(cr[None] * p1i[:, :, None, :] + ci[None] * p1r[:, :, None, :])

    def chunk_out(m):
        return (m.reshape(j, nblk, gb, h, p).transpose(1, 4, 0, 2, 3)
                .reshape(nblk, p, j * gb * h))

    a16r = pr[j].reshape(nblk, 1, gb * p)
    a16i = pi[j].reshape(nblk, 1, gb * p)
    return (t_op.astype(BF16), chunk_in(er).astype(BF16), chunk_in(ei).astype(BF16),
            chunk_out(cc_r).astype(BF16), chunk_out(cc_i).astype(BF16), a16r, a16i)


def _s5_kernel(u_ref, t_ref, bre_ref, bim_ref, cre_ref, cim_ref, ar_ref, ai_ref, mt_ref, mb_ref,
               o_ref, tblk, bblk_r, bblk_i, cblk_r, cblk_i, zre, zim, pre, pim,
               *, chunks_per_batch, n_batch):
    gb = S5_GROUP_BLOCK
    j = S5_CHUNK
    hh = t_ref.shape[2]
    pp = cre_ref.shape[1]
    lanes = gb * hh
    ncol = 4 * lanes
    for i in range(j):
        j_first = (i * lanes // ncol) * ncol // lanes
        for gl in range(gb):
            src = slice(i * hh, (i + 1) * hh)
            dst = slice(i * lanes + gl * hh, i * lanes + (gl + 1) * hh)
            m_gl = mt_ref[gl, :, 0:lanes]
            for jj in range(j_first, j):
                blk = t_ref[0, jj - i] * m_gl if jj >= i else jnp.zeros((hh, lanes), tblk.dtype)
                tblk[dst, jj * lanes:(jj + 1) * lanes] = blk
            bblk_r[dst, :] = bre_ref[0, src, :] * mb_ref[gl]
            bblk_i[dst, :] = bim_ref[0, src, :] * mb_ref[gl]
    for gl in range(gb):
        for r0 in range(0, pp, hh):
            src = slice(r0, r0 + hh)
            dst = slice(gl * pp + r0, gl * pp + r0 + hh)
            cblk_r[dst, :] = cre_ref[0, src, :] * mt_ref[gl]
            cblk_i[dst, :] = cim_ref[0, src, :] * mt_ref[gl]

    u = u_ref[0]
    zre[...] = jnp.dot(u, bblk_r[...], preferred_element_type=F32)
    zim[...] = jnp.dot(u, bblk_i[...], preferred_element_type=F32)
    ar = ar_ref[0]
    ai = ai_ref[0]

    def step(c, carry):
        new = []
        for b in range(n_batch):
            sr, si = carry[2 * b], carry[2 * b + 1]
            row = pl.ds(b * chunks_per_batch + c, 1)
            pre[row, :] = sr
            pim[row, :] = si
            new.append(ar * sr - ai * si + zre[row, :])
            new.append(ar * si + ai * sr + zim[row, :])
        return tuple(new)

    zero = jnp.zeros((1, zre.shape[1]), F32)
    lax.fori_loop(0, chunks_per_batch, step, (zero,) * (2 * n_batch), unroll=8)

    pr_b = pre[...].astype(BF16)
    pi_b = pim[...].astype(BF16)
    for c0 in range(0, j * lanes, ncol):
        cols = slice(c0, c0 + ncol)
        k_rows = c0 + ncol
        y = jnp.dot(u[:, :k_rows], tblk[:k_rows, cols], preferred_element_type=F32)
        y += jnp.dot(pr_b, cblk_r[:, cols], preferred_element_type=F32)
        y += jnp.dot(pi_b, cblk_i[:, cols], preferred_element_type=F32)
        o_ref[0, :, cols] = jax.nn.gelu(y).astype(o_ref.dtype)


def _s5_core(ub, ops, n_batch):
    t_op, bre, bim, cre, cim, a16r, a16i = ops
    nblk, t, lanes = ub.shape
    j = S5_CHUNK
    gb = S5_GROUP_BLOCK
    hh = lanes // gb
    nchunk = t // j
    pp = cre.shape[1]
    uc = ub.reshape(nblk, nchunk, j * lanes)
    def own(width, per_group):
        col_group = jnp.arange(width) // per_group % gb
        m = col_group[None, None, :] == jnp.arange(gb)[:, None, None]
        return jnp.broadcast_to(m, (gb, hh, width)).astype(BF16)

    mask_t = own(j * lanes, hh)
    mask_b = own(gb * pp, pp)
    blk = lambda r, c: pl.BlockSpec((1, r, c), lambda i: (i, 0, 0))
    const = lambda a: pl.BlockSpec(a.shape, lambda i: (0, 0, 0))
    yc = pl.pallas_call(
        functools.partial(_s5_kernel, chunks_per_batch=nchunk // n_batch, n_batch=n_batch),
        grid=(nblk,),
        in_specs=[blk(nchunk, j * lanes), pl.BlockSpec((1, j, hh, lanes), lambda i: (i, 0, 0, 0)),
                  blk(j * hh, gb * pp), blk(j * hh, gb * pp),
                  blk(pp, j * lanes), blk(pp, j * lanes),
                  blk(1, gb * pp), blk(1, gb * pp), const(mask_t), const(mask_b)],
        out_specs=blk(nchunk, j * lanes),
        out_shape=jax.ShapeDtypeStruct((nblk, nchunk, j * lanes), BF16),
        scratch_shapes=[pltpu.VMEM((j * lanes, j * lanes), BF16),
                        pltpu.VMEM((j * lanes, gb * pp), BF16), pltpu.VMEM((j * lanes, gb * pp), BF16),
                        pltpu.VMEM((gb * pp, j * lanes), BF16), pltpu.VMEM((gb * pp, j * lanes), BF16)]
                       + [pltpu.VMEM((nchunk, gb * pp), F32)] * 4,
        compiler_params=_params(("parallel",)),
        name="s5_chunked_scan",
    )(uc, t_op, bre, bim, cre, cim, a16r, a16i, mask_t, mask_b)
    return yc.reshape(nblk, t, lanes)


def _glu_norm_kernel(g_ref, w_ref, gn_ref, o_ref):
    g = jnp.concatenate([g_ref[i] for i in range(g_ref.shape[0])], axis=1)
    gate = jax.nn.sigmoid(jnp.dot(g, w_ref[...], preferred_element_type=F32))
    y = g.astype(F32) * gate
    y = y * lax.rsqrt(jnp.mean(y * y, axis=-1, keepdims=True) + EPS)
    o_ref[...] = (y * gn_ref[...]).astype(o_ref.dtype)


def _glu_norm(gb_, w_glu, layer, gn, tm=512):
    nblk, t, lanes = gb_.shape
    w = nblk * lanes
    return pl.pallas_call(
        _glu_norm_kernel,
        grid=(t // tm,),
        in_specs=[pl.BlockSpec((nblk, tm, lanes), lambda i: (0, i, 0)),
                  pl.BlockSpec((None, w, w), lambda i: (layer, 0, 0)),
                  pl.BlockSpec((1, w), lambda i: (0, 0))],
        out_specs=pl.BlockSpec((tm, w), lambda i: (i, 0)),
        out_shape=jax.ShapeDtypeStruct((t, w), BF16),
        compiler_params=_params(("parallel",)),
        name="glu_groupnorm",
    )(gb_, w_glu, gn.astype(F32).reshape(1, w))


def _head_rms_scale(x, sel_ref, exp_ref, head_dim):
    ss = jnp.dot((x * x).astype(BF16), sel_ref[...], preferred_element_type=F32)
    r = lax.rsqrt(ss * (1.0 / head_dim) + EPS)
    r_hi = r.astype(BF16)
    r_lo = (r - r_hi.astype(F32)).astype(BF16)
    return (jnp.dot(r_hi, exp_ref[...], preferred_element_type=F32)
            + jnp.dot(r_lo, exp_ref[...], preferred_element_type=F32))


def _attn_kernel(sink_ref, q_ref, kp_ref, kc_ref, vp_ref, vc_ref, qg_ref, kg_ref, gn_ref,
                 qsel_ref, qexp_ref, ksel_ref, kexp_ref, o_ref, acc,
                 *, n_q_heads, n_kv_heads, head_dim):
    nblk = pl.program_id(1)
    w = WINDOW
    lw = 2 * head_dim
    grp = n_q_heads // n_kv_heads
    t_loc = lax.broadcasted_iota(jnp.int32, (w, 2 * w), 0)
    s_loc = lax.broadcasted_iota(jnp.int32, (w, 2 * w), 1)
    dist = t_loc + w - s_loc
    valid = (dist >= 0) & (dist < w) & (s_loc + nblk * w >= w)
    dmask = jnp.where(valid, dist.astype(F32), jnp.inf)

    q = q_ref[...].astype(F32)
    qn = (q * _head_rms_scale(q, qsel_ref, qexp_ref, head_dim) * qg_ref[...]).astype(BF16)
    k = jnp.concatenate([kp_ref[...], kc_ref[...]], axis=0).astype(F32)
    kn = k * _head_rms_scale(k, ksel_ref, kexp_ref, head_dim) * kg_ref[...]
    v = jnp.concatenate([vp_ref[...], vc_ref[...]], axis=0).astype(F32)

    lane = lax.broadcasted_iota(jnp.int32, (2 * w, lw), 1)
    lo = lane < head_dim

    def block_diag(x, hk):
        grp_lanes = x[:, (hk // 2) * lw:(hk // 2 + 1) * lw]
        swapped = pltpu.roll(grp_lanes, head_dim, 1)
        low, high = (grp_lanes, swapped) if hk % 2 == 0 else (swapped, grp_lanes)
        return jnp.concatenate([jnp.where(lo, low, 0.0), jnp.where(lo, 0.0, high)],
                               axis=0).astype(BF16)

    lane_q = lax.broadcasted_iota(jnp.int32, (w, lw), 1) < head_dim
    for hk in range(n_kv_heads):
        kb = block_diag(kn, hk)
        vb = block_diag(v, hk)
        for pq in range(grp // 2):
            pair = hk * (grp // 2) + pq
            cols = slice(pair * lw, (pair + 1) * lw)
            s = lax.dot_general(qn[:, cols], kb, (((1,), (1,)), ((), ())),
                                preferred_element_type=F32)
            es, inv = [], []
            for i in range(2):
                hq = 2 * pair + i
                slope = 2.0 ** (-8.0 * (hq + 1) / n_q_heads)
                si = s[:, i * 2 * w:(i + 1) * 2 * w] - slope * dmask
                sink = sink_ref[hq]
                m = jnp.maximum(jnp.max(si, axis=-1, keepdims=True), sink)
                e = jnp.exp(si - m)
                inv.append(1.0 / (jnp.sum(e, axis=-1, keepdims=True) + jnp.exp(sink - m)))
                es.append(e.astype(BF16))
            pv = jnp.dot(jnp.concatenate(es, axis=1), vb, preferred_element_type=F32)
            acc[:, cols] = pv * jnp.where(lane_q, inv[0], inv[1])
    y = acc[...]
    y = y * lax.rsqrt(jnp.mean(y * y, axis=-1, keepdims=True) + EPS)
    o_ref[...] = (y * gn_ref[...]).astype(o_ref.dtype)


def _attention(z, ssm_w, attn_w, kv_w, q_gain, k_gain, sinks, gn, n_batch):
    t = z.shape[0]
    w = WINDOW
    head_dim = q_gain.shape[0]
    n_q = attn_w // head_dim
    n_kv = kv_w // head_dim
    nb = t // n_batch // w
    qcol = ssm_w // attn_w
    kcol = (ssm_w + attn_w) // kv_w
    vcol = kcol + 1
    cur = lambda col: (lambda b, n: (b * nb + n, col))
    prev = lambda col: (lambda b, n: (b * nb + jnp.maximum(n - 1, 0), col))
    full = lambda r, c: pl.BlockSpec((r, c), lambda b, n: (0, 0))
    nsel = 128

    def selectors(width):
        sel = (jnp.arange(width)[:, None] // head_dim == jnp.arange(nsel)[None, :]).astype(BF16)
        return sel, sel.T

    qsel, qexp = selectors(attn_w)
    ksel, kexp = selectors(kv_w)
    qg_row = (jnp.tile(q_gain.astype(F32), n_q) * head_dim ** -0.5).reshape(1, attn_w)
    kg_row = jnp.tile(k_gain.astype(F32), n_kv).reshape(1, kv_w)
    return pl.pallas_call(
        functools.partial(_attn_kernel, n_q_heads=n_q, n_kv_heads=n_kv, head_dim=head_dim),
        grid=(n_batch, nb),
        in_specs=[pl.BlockSpec(memory_space=pltpu.SMEM),
                  pl.BlockSpec((w, attn_w), cur(qcol)),
                  pl.BlockSpec((w, kv_w), prev(kcol)),
                  pl.BlockSpec((w, kv_w), cur(kcol)),
                  pl.BlockSpec((w, kv_w), prev(vcol)),
                  pl.BlockSpec((w, kv_w), cur(vcol)),
                  full(1, attn_w), full(1, kv_w), full(1, attn_w),
                  full(attn_w, nsel), full(nsel, attn_w), full(kv_w, nsel), full(nsel, kv_w)],
        out_specs=pl.BlockSpec((w, attn_w), lambda b, n: (b * nb + n, 0)),
        out_shape=jax.ShapeDtypeStruct((t, attn_w), BF16),
        scratch_shapes=[pltpu.VMEM((w, attn_w), F32)],
        compiler_params=_params(("parallel", "parallel")),
        name="swa_attention",
    )(sinks.astype(F32), z, z, z, z, z, qg_row, kg_row, gn.astype(F32).reshape(1, attn_w),
      qsel, qexp, ksel, kexp)


def _kth_largest_rows(vals, k):
    out = []
    work = vals
    for _ in range(k):
        m = jnp.max(work, axis=0, keepdims=True)
        out.append(m)
        work = jnp.where(work == m, -jnp.inf, work)
    return out


def _kth_largest_value(vals, k):
    work = vals
    left = jnp.full((1, vals.shape[1]), float(k), F32)
    kth = jnp.full((1, vals.shape[1]), -jnp.inf, F32)
    for _ in range(k):
        m = jnp.max(work, axis=0, keepdims=True)
        hit = work == m
        kth = jnp.where(left > 0.0, m, kth)
        left = left - jnp.sum(jnp.where(hit, 1.0, 0.0), axis=0, keepdims=True)
        work = jnp.where(hit, -jnp.inf, work)
    return kth


def _staircase_candidates(v1, v2):
    k = PEER_TOPK + 1
    sub = 8
    v1m = jnp.concatenate(v1[:2 * sub], axis=0)
    v2m = jnp.concatenate(v2[:2 * sub], axis=0)
    row = lax.broadcasted_iota(jnp.int32, (sub, v2m.shape[1]), 0)
    blocks = [v1[0] + v2m]
    for i in range(1, sub):
        keep = k // (i + 1)
        blk = v1[i] + v2m[0:sub]
        blocks.append(blk if keep >= sub else jnp.where(row < keep, blk, -jnp.inf))
    blocks.append(v1m[sub:] + v2[0])
    pad = jnp.full((sub - 2, v2m.shape[1]), -jnp.inf, F32)
    blocks.append(jnp.concatenate([v1[2 * sub] + v2[0], v1[0] + v2[2 * sub], pad], axis=0))
    return jnp.concatenate(blocks, axis=0)


def _peer_topk_kernel(qt_ref, k1_ref, k2_ref, s1_ref, s2_ref, st_ref, *, n_heads, n_keys):
    half = k1_ref.shape[1]
    hp = lax.Precision.HIGHEST
    k1 = k1_ref[...]
    k2 = k2_ref[...]
    taus = []
    for h in range(n_heads):
        base = h * 2 * half
        s1 = jnp.dot(k1, qt_ref[base:base + half, :], preferred_element_type=F32, precision=hp)
        s2 = jnp.dot(k2, qt_ref[base + half:base + 2 * half, :], preferred_element_type=F32,
                     precision=hp)
        a1 = s1 * LOG2E
        a2 = s2 * LOG2E
        v1 = _kth_largest_rows(a1, PEER_TOPK + 1)
        v2 = _kth_largest_rows(a2, PEER_TOPK + 1)
        v1s = [v - v1[0] for v in v1]
        v2s = [v - v2[0] for v in v2]
        cand = _staircase_candidates(v1s, v2s)
        picked = cand >= _kth_largest_value(cand, PEER_TOPK)
        zsum = jnp.sum(jnp.where(picked, jnp.exp2(cand), 0.0), axis=0, keepdims=True)
        shift1 = v1[0] + jnp.log2(zsum)
        cand_z = _staircase_candidates([v - shift1 for v in v1], v2s)
        last_in = jnp.min(jnp.where(picked, cand_z, jnp.inf), axis=0, keepdims=True)
        first_out = jnp.max(jnp.where(picked, -jnp.inf, cand_z), axis=0, keepdims=True)
        taus.append(0.5 * (last_in + first_out))
        s1_ref[h * n_keys:(h + 1) * n_keys, :] = a1 - shift1
        s2_ref[h * n_keys:(h + 1) * n_keys, :] = a2 - v2[0]
    st_ref[...] = jnp.concatenate(taus, axis=0)


def _peer_topk(qt, k1, k2, tn=256):
    hq, t = qt.shape
    n_keys, half = k1.shape
    n_heads = hq // (2 * half)
    srows = n_heads * n_keys
    col = lambda rows: pl.BlockSpec((rows, tn), lambda i: (0, i))
    return pl.pallas_call(
        functools.partial(_peer_topk_kernel, n_heads=n_heads, n_keys=n_keys),
        grid=(t // tn,),
        in_specs=[col(hq),
                  pl.BlockSpec((n_keys, half), lambda i: (0, 0)),
                  pl.BlockSpec((n_keys, half), lambda i: (0, 0))],
        out_specs=[col(srows), col(srows), col(n_heads)],
        out_shape=[jax.ShapeDtypeStruct((srows, t), F32), jax.ShapeDtypeStruct((srows, t), F32),
                   jax.ShapeDtypeStruct((n_heads, t), F32)],
        compiler_params=_params(("parallel",)),
        name="peer_topk",
    )(qt, k1.astype(F32), k2.astype(F32))


def _peer_dense_kernel(h_ref, u_ref, v_ref, l1_ref, l2_ref, tau_ref, o_ref, tau8, l1b, wt,
                       *, n_heads, n_keys):
    j = pl.program_id(1)
    te = v_ref.shape[0]
    tm = h_ref.shape[0]
    d = v_ref.shape[1]
    na = te // n_keys
    sub = 8
    dc = 1024

    @pl.when(j == 0)
    def _():
        o_ref[...] = jnp.zeros_like(o_ref)
        for h in range(n_heads):
            tau8[h * sub:(h + 1) * sub, :] = jnp.broadcast_to(tau_ref[h:h + 1, :], (sub, tm))

    for aa in range(na):
        for h in range(n_heads):
            row = l1_ref[pl.ds(h * n_keys + j * na + aa, 1), :]
            k = (aa * n_heads + h) * sub
            l1b[k:k + sub, :] = jnp.broadcast_to(row, (sub, tm))

    ax = 2
    link = jnp.zeros((sub, tm), F32)
    for a0 in range(0, na, ax):
        for b0 in range(0, n_keys, 2 * sub):
            acc = [[link, link] for _ in range(ax)]
            for h in range(n_heads):
                t8 = tau8[h * sub:(h + 1) * sub, :]
                r2 = [slice(h * n_keys + b0 + y * sub, h * n_keys + b0 + (y + 1) * sub) for y in range(2)]
                l2v = [l2_ref[r, :] for r in r2]
                for x in range(ax):
                    k = ((a0 + x) * n_heads + h) * sub
                    l1v = l1b[k:k + sub, :]
                    for y in range(2):
                        logit = l1v + l2v[y]
                        acc[x][y] = acc[x][y] + jnp.exp2(jnp.where(logit >= t8, logit, -jnp.inf))
            for x in range(ax):
                r0 = (a0 + x) * n_keys + b0
                wt[r0:r0 + 2 * sub, :] = jnp.concatenate(acc[x], axis=0).astype(wt.dtype)
            link = jnp.minimum(pltpu.roll(acc[0][0], 1, 1), 0.0)

    w = wt[...].T
    act = lax.dot_general(h_ref[...], u_ref[...], (((1,), (1,)), ((), ())),
                          preferred_element_type=F32)
    g = jax.nn.gelu(act).astype(BF16) * w
    for c0 in range(0, d, dc):
        o_ref[:, c0:c0 + dc] += jnp.dot(g, v_ref[:, c0:c0 + dc], preferred_element_type=F32)


def _peer_dense(h2, u_tab, v_tab, layer, s1t, s2t, stats, tm=512, te=512):
    t, d = h2.shape
    ne = v_tab.shape[1]
    n_keys = int(round(math.sqrt(ne)))
    n_heads = s1t.shape[0] // n_keys
    srows = s1t.shape[0]
    na = te // n_keys
    tok = lambda rows: pl.BlockSpec((rows, tm), lambda i, j: (0, i))
    return pl.pallas_call(
        functools.partial(_peer_dense_kernel, n_heads=n_heads, n_keys=n_keys),
        grid=(t // tm, ne // te),
        in_specs=[pl.BlockSpec((tm, d), lambda i, j: (i, 0)),
                  pl.BlockSpec((None, te, d), lambda i, j: (layer, j, 0)),
                  pl.BlockSpec((None, te, d), lambda i, j: (layer, j, 0)),
                  tok(srows), tok(srows), tok(n_heads)],
        out_specs=pl.BlockSpec((tm, d), lambda i, j: (i, 0)),
        out_shape=jax.ShapeDtypeStruct((t, d), F32),
        scratch_shapes=[pltpu.VMEM((n_heads * 8, tm), F32),
                        pltpu.VMEM((na * n_heads * 8, tm), F32),
                        pltpu.VMEM((te, tm), BF16)],
        compiler_params=_params(("parallel", "arbitrary")),
        name="peer_dense",
    )(h2, u_tab, v_tab, s1t, s2t, stats)


def kernel(x, c, w_ada, b_ada, ada_layer, norm1_g, norm2_g, w_in, lam_re, lam_im, log_dt, b_re, b_im,
           c_re, c_im, d_skip, w_glu, q_gain, k_gain, sinks, gn_ssm, gn_attn, w_out, peer_wq, peer_k1,
           peer_k2, peer_u, peer_v):
    bsz, seq, d = x.shape
    t = bsz * seq
    depth = w_in.shape[0]
    n_mod = ada_layer.shape[1]
    ssm_w = w_glu.shape[1]
    attn_w = gn_attn.shape[1]
    kv_w = (w_in.shape[2] - ssm_w - attn_w) // 2

    w_in_b, w_glu_b, w_out_b = w_in.astype(BF16), w_glu.astype(BF16), w_out.astype(BF16)
    wq_t_b = peer_wq.transpose(0, 2, 1).astype(BF16)
    u_b, v_b = peer_u.astype(BF16), peer_v.astype(BF16)

    cond = _cond(c, w_ada, b_ada).reshape(bsz, n_mod, d)
    xf = x.astype(F32).reshape(t, d)
    delta, gate_prev = None, None
    for l in range(depth):
        mod = cond + ada_layer[l].astype(F32)
        shift1, scale1, gate1, shift2, scale2, gate2 = (mod[:, i] for i in range(n_mod))

        xf, h = _adaln_norm(xf, delta, gate_prev, norm1_g[l], scale1, shift1, seq)
        z_ssm = _matmul_channel_blocks(h, w_in_b, l, ssm_w, BF16, tm=1024, tn=512)
        z_attn = _matmul(h, w_in_b, l, ssm_w, attn_w + 2 * kv_w, BF16, tm=1024, tn=512)
        ops = _s5_operators(lam_re[l], lam_im[l], log_dt[l], b_re[l], b_im[l], c_re[l], c_im[l],
                            d_skip[l])
        g = _s5_core(z_ssm, ops, bsz)
        y_ssm = _glu_norm(g, w_glu_b, l, gn_ssm[l])
        y_attn = _attention(z_attn, 0, attn_w, kv_w, q_gain[l], k_gain[l], sinks[l], gn_attn[l], bsz)
        mixed = _matmul_concat(y_ssm, y_attn, w_out_b, l, BF16, tm=1024, tn=512)

        xf, h2 = _adaln_norm(xf, mixed, gate1, norm2_g[l], scale2, shift2, seq)
        qt = _matmul_nt(wq_t_b, l, h2, F32, tn=512)
        s1t, s2t, stats = _peer_topk(qt, peer_k1[l], peer_k2[l])
        delta = _peer_dense(h2, u_b, v_b, l, s1t, s2t, stats)
        gate_prev = gate2
    out = _residual_add(xf, delta, gate_prev, seq)
    return out.reshape(bsz, seq, d).astype(x.dtype)
```

```python
import functools
import math

import jax
import jax.numpy as jnp
from jax import lax
from jax.experimental import pallas as pl
from jax.experimental.pallas import tpu as pltpu

F32 = jnp.float32
BF16 = jnp.bfloat16
EPS = 1e-6
WINDOW = 128
PEER_TOPK = 16
LOG2E = 1.4426950408889634
S5_CHUNK = 16
S5_GROUP_BLOCK = 8
V7X_VMEM_LIMIT = 56 * 1024 * 1024


def _params(semantics, vmem=V7X_VMEM_LIMIT, flags=None):
    return pltpu.CompilerParams(dimension_semantics=semantics, vmem_limit_bytes=vmem, flags=flags)


def _mm_kernel(a_ref, b_ref, o_ref):
    o_ref[...] = jnp.dot(a_ref[...], b_ref[...].astype(a_ref.dtype),
                         preferred_element_type=F32).astype(o_ref.dtype)


def _matmul(a, b, layer, col0, n, out_dtype, tm, tn):
    m, k = a.shape
    j0 = col0 // tn
    return pl.pallas_call(
        _mm_kernel,
        grid=(m // tm, n // tn),
        in_specs=[pl.BlockSpec((tm, k), lambda i, j: (i, 0)),
                  pl.BlockSpec((None, k, tn), lambda i, j: (layer, 0, j0 + j))],
        out_specs=pl.BlockSpec((tm, tn), lambda i, j: (i, j)),
        out_shape=jax.ShapeDtypeStruct((m, n), out_dtype),
        compiler_params=_params(("parallel", "parallel")),
        name="matmul",
    )(a, b)


def _mm_cb_kernel(a_ref, b_ref, o_ref, scr):
    res = jnp.dot(a_ref[...], b_ref[...].astype(a_ref.dtype), preferred_element_type=F32)
    nb, rows, width = o_ref.shape
    lanes = scr.shape[2]
    steps = width // lanes
    for k in range(nb):
        scr[k] = res[:, k * lanes:(k + 1) * lanes]
    for k in range(nb):
        for s in range(steps):
            o_ref[k, :, s * lanes:(s + 1) * lanes] = (
                scr[k, pl.ds(s, rows, stride=steps), :].astype(o_ref.dtype))


def _matmul_channel_blocks(a, b, layer, n, out_dtype, tm, tn, steps, lanes=128):
    m, k = a.shape
    return pl.pallas_call(
        _mm_cb_kernel,
        grid=(m // tm, n // tn),
        in_specs=[pl.BlockSpec((tm, k), lambda i, j: (i, 0)),
                  pl.BlockSpec((None, k, tn), lambda i, j: (layer, 0, j))],
        out_specs=pl.BlockSpec((tn // lanes, tm // steps, steps * lanes), lambda i, j: (j, i, 0)),
        out_shape=jax.ShapeDtypeStruct((n // lanes, m // steps, steps * lanes), out_dtype),
        scratch_shapes=[pltpu.VMEM((tn // lanes, tm, lanes), F32)],
        compiler_params=_params(("parallel", "parallel")),
        name="matmul_channel_blocks",
    )(a, b)


def _mm2_kernel(a1_ref, a2_ref, b1_ref, b2_ref, o_ref):
    acc = jnp.dot(a1_ref[...], b1_ref[...].astype(a1_ref.dtype), preferred_element_type=F32)
    acc += jnp.dot(a2_ref[...], b2_ref[...].astype(a2_ref.dtype), preferred_element_type=F32)
    o_ref[...] = acc.astype(o_ref.dtype)


def _matmul_concat(a1, a2, b, layer, out_dtype, tm, tn):
    m, k1 = a1.shape
    k2 = a2.shape[1]
    assert k1 == k2
    n = b.shape[2]
    return pl.pallas_call(
        _mm2_kernel,
        grid=(m // tm, n // tn),
        in_specs=[pl.BlockSpec((tm, k1), lambda i, j: (i, 0)),
                  pl.BlockSpec((tm, k2), lambda i, j: (i, 0)),
                  pl.BlockSpec((None, k1, tn), lambda i, j: (layer, 0, j)),
                  pl.BlockSpec((None, k2, tn), lambda i, j: (layer, 1, j))],
        out_specs=pl.BlockSpec((tm, tn), lambda i, j: (i, j)),
        out_shape=jax.ShapeDtypeStruct((m, n), out_dtype),
        compiler_params=_params(("parallel", "parallel")),
        name="matmul_concat",
    )(a1, a2, b, b)


def _mm_nt_kernel(a_ref, b_ref, o_ref):
    o_ref[...] = lax.dot_general(a_ref[...], b_ref[...], (((1,), (1,)), ((), ())),
                                 preferred_element_type=F32).astype(o_ref.dtype)


def _matmul_nt(a, layer, b, out_dtype, tn):
    _, m, k = a.shape
    n = b.shape[0]
    return pl.pallas_call(
        _mm_nt_kernel,
        grid=(n // tn,),
        in_specs=[pl.BlockSpec((None, m, k), lambda j: (layer, 0, 0)),
                  pl.BlockSpec((tn, k), lambda j: (j, 0))],
        out_specs=pl.BlockSpec((m, tn), lambda j: (0, j)),
        out_shape=jax.ShapeDtypeStruct((m, n), out_dtype),
        compiler_params=_params(("parallel",)),
        name="matmul_nt",
    )(a, b)


def _cond_kernel(c_ref, w_ref, b_ref, o_ref):
    c = c_ref[...]
    s = c * jax.nn.sigmoid(c)
    o_ref[...] = jnp.dot(s, w_ref[...], preferred_element_type=F32,
                         precision=lax.Precision.HIGHEST) + b_ref[...]


def _cond(c, w_ada, b_ada, tn=512):
    bsz, d = c.shape
    n = w_ada.shape[1]
    rows = 8
    cp = jnp.zeros((rows, d), F32).at[:bsz].set(c.astype(F32))
    out = pl.pallas_call(
        _cond_kernel,
        grid=(n // tn,),
        in_specs=[pl.BlockSpec((rows, d), lambda j: (0, 0)),
                  pl.BlockSpec((d, tn), lambda j: (0, j)),
                  pl.BlockSpec((1, tn), lambda j: (0, j))],
        out_specs=pl.BlockSpec((rows, tn), lambda j: (0, j)),
        out_shape=jax.ShapeDtypeStruct((rows, n), F32),
        compiler_params=_params(("parallel",)),
        name="adaln_cond",
    )(cp, w_ada.astype(F32), b_ada.astype(F32).reshape(1, n))
    return out[:bsz]


def _norm_body(x, g_ref, scale_ref, shift_ref, h_ref):
    y = x * lax.rsqrt(jnp.mean(x * x, axis=-1, keepdims=True) + EPS)
    y = y * g_ref[...]
    h_ref[...] = (y * (1.0 + scale_ref[0]) + shift_ref[0]).astype(h_ref.dtype)


def _norm_kernel(x_ref, g_ref, scale_ref, shift_ref, h_ref):
    _norm_body(x_ref[...], g_ref, scale_ref, shift_ref, h_ref)


def _resnorm_kernel(x_ref, d_ref, gate_ref, g_ref, scale_ref, shift_ref, xo_ref, h_ref):
    x = x_ref[...] + gate_ref[0] * d_ref[...]
    xo_ref[...] = x
    _norm_body(x, g_ref, scale_ref, shift_ref, h_ref)


def _adaln_norm(x, delta, gate, g, scale, shift, rows_per_batch, tm=256):
    t, d = x.shape
    tpb = rows_per_batch // tm
    row = pl.BlockSpec((tm, d), lambda i: (i, 0))
    per_batch = pl.BlockSpec((1, 1, d), lambda i: (i // tpb, 0, 0))
    vec = pl.BlockSpec((1, d), lambda i: (0, 0))
    g2 = g.astype(F32).reshape(1, d)
    b3 = lambda v: v.astype(F32).reshape(v.shape[0], 1, d)
    if delta is None:
        h = pl.pallas_call(
            _norm_kernel,
            grid=(t // tm,),
            in_specs=[row, vec, per_batch, per_batch],
            out_specs=row,
            out_shape=jax.ShapeDtypeStruct((t, d), BF16),
            compiler_params=_params(("parallel",)),
            name="adaln_norm",
        )(x, g2, b3(scale), b3(shift))
        return x, h
    return pl.pallas_call(
        _resnorm_kernel,
        grid=(t // tm,),
        in_specs=[row, row, per_batch, vec, per_batch, per_batch],
        out_specs=[row, row],
        out_shape=[jax.ShapeDtypeStruct((t, d), F32), jax.ShapeDtypeStruct((t, d), BF16)],
        compiler_params=_params(("parallel",)),
        name="residual_adaln_norm",
    )(x, delta, b3(gate), g2, b3(scale), b3(shift))


def _residual_kernel(x_ref, d_ref, gate_ref, o_ref):
    o_ref[...] = x_ref[...] + gate_ref[0] * d_ref[...]


def _residual_add(x, delta, gate, rows_per_batch, tm=256):
    t, d = x.shape
    tpb = rows_per_batch // tm
    row = pl.BlockSpec((tm, d), lambda i: (i, 0))
    return pl.pallas_call(
        _residual_kernel,
        grid=(t // tm,),
        in_specs=[row, row, pl.BlockSpec((1, 1, d), lambda i: (i // tpb, 0, 0))],
        out_specs=row,
        out_shape=jax.ShapeDtypeStruct((t, d), F32),
        compiler_params=_params(("parallel",)),
        name="residual_add",
    )(x, delta, gate.astype(F32).reshape(gate.shape[0], 1, d))


def _s5_operators(lam_re, lam_im, log_dt, b_re, b_im, c_re, c_im, d_skip):
    hp = lax.Precision.HIGHEST
    j = S5_CHUNK
    f = lambda v: v.astype(F32)
    lr, li = f(lam_re), f(lam_im)
    g, p = lr.shape
    h = b_re.shape[-1]
    dt = jnp.exp(f(log_dt))[:, None]
    mag = jnp.exp(lr * dt)
    ar = mag * jnp.cos(li * dt)
    ai = mag * jnp.sin(li * dt)
    den = lr * lr + li * li
    kr = ((ar - 1.0) * lr + ai * li) / den
    ki = (ai * lr - (ar - 1.0) * li) / den
    br, bi = f(b_re), f(b_im)
    bbr = kr[..., None] * br - ki[..., None] * bi
    bbi = kr[..., None] * bi + ki[..., None] * br
    cr, ci = f(c_re), f(c_im)
    pr, pi = [jnp.ones_like(ar)], [jnp.zeros_like(ar)]
    for _ in range(j):
        pr.append(pr[-1] * ar - pi[-1] * ai)
        pi.append(pr[-2] * ai + pi[-1] * ar)
    pr, pi = jnp.stack(pr), jnp.stack(pi)
    er = pr[:j, :, :, None] * bbr - pi[:j, :, :, None] * bbi
    ei = pr[:j, :, :, None] * bbi + pi[:j, :, :, None] * bbr
    kk = (jnp.einsum('ghp,ngpk->nghk', cr, er, precision=hp)
          - jnp.einsum('ghp,ngpk->nghk', ci, ei, precision=hp))
    kk = kk.at[0].add(f(d_skip)[:, :, None] * jnp.eye(h, dtype=F32))
    gb = S5_GROUP_BLOCK
    nblk = g // gb
    t_op = kk.reshape(j, nblk, gb, h, h).transpose(1, 0, 4, 2, 3).reshape(nblk, j, h, gb * h)

    def chunk_in(e):
        return (e[::-1].reshape(j, nblk, gb, p, h).transpose(1, 0, 4, 2, 3)
                .reshape(nblk, j * h, gb * p))

    p1r, p1i = pr[1:], pi[1:]
    cc_r = cr[None] * p1r[:, :, None, :] - ci[None] * p1i[:, :, None, :]
    cc_i = -(cr[None] * p1i[:, :, None, :] + ci[None] * p1r[:, :, None, :])

    def chunk_out(m):
        return (m.reshape(j, nblk, gb, h, p).transpose(1, 4, 0, 2, 3)
                .reshape(nblk, p, j * gb * h))

    a16r = pr[j].reshape(nblk, 1, gb * p)
    a16i = pi[j].reshape(nblk, 1, gb * p)
    return (t_op.astype(BF16), chunk_in(er).astype(BF16), chunk_in(ei).astype(BF16),
            chunk_out(cc_r).astype(BF16), chunk_out(cc_i).astype(BF16), a16r, a16i)


def _s5_kernel(u_ref, t_ref, bre_ref, bim_ref, cre_ref, cim_ref, ar_ref, ai_ref, mt_ref, mb_ref,
               o_ref, tblk, bblk_r, bblk_i, cblk_r, cblk_i, zre, zim, pre, pim,
               *, chunks_per_batch, n_batch):
    gb = S5_GROUP_BLOCK
    j = S5_CHUNK
    hh = t_ref.shape[2]
    pp = cre_ref.shape[1]
    lanes = gb * hh
    ncol = 4 * lanes
    for i in range(j):
        j_first = (i * lanes // ncol) * ncol // lanes
        for gl in range(gb):
            src = slice(i * hh, (i + 1) * hh)
            dst = slice(i * lanes + gl * hh, i * lanes + (gl + 1) * hh)
            m_gl = mt_ref[gl, :, 0:lanes]
            for jj in range(j_first, j):
                blk = t_ref[0, jj - i] * m_gl if jj >= i else jnp.zeros((hh, lanes), tblk.dtype)
                tblk[dst, jj * lanes:(jj + 1) * lanes] = blk
            bblk_r[dst, :] = bre_ref[0, src, :] * mb_ref[gl]
            bblk_i[dst, :] = bim_ref[0, src, :] * mb_ref[gl]
    for gl in range(gb):
        for r0 in range(0, pp, hh):
            src = slice(r0, r0 + hh)
            dst = slice(gl * pp + r0, gl * pp + r0 + hh)
            cblk_r[dst, :] = cre_ref[0, src, :] * mt_ref[gl]
            cblk_i[dst, :] = cim_ref[0, src, :] * mt_ref[gl]

    u = u_ref[0]
    zre[...] = jnp.dot(u, bblk_r[...], preferred_element_type=F32)
    zim[...] = jnp.dot(u, bblk_i[...], preferred_element_type=F32)
    ar = ar_ref[0]
    ai = ai_ref[0]

    def step(c, carry):
        new = []
        for b in range(n_batch):
            sr, si = carry[2 * b], carry[2 * b + 1]
            row = pl.ds(b * chunks_per_batch + c, 1)
            pre[row, :] = sr
            pim[row, :] = si
            new.append(ar * sr - ai * si + zre[row, :])
            new.append(ar * si + ai * sr + zim[row, :])
        return tuple(new)

    zero = jnp.zeros((1, zre.shape[1]), F32)
    lax.fori_loop(0, chunks_per_batch, step, (zero,) * (2 * n_batch), unroll=8)

    pr_b = pre[...].astype(BF16)
    pi_b = pim[...].astype(BF16)
    for c0 in range(0, j * lanes, ncol):
        cols = slice(c0, c0 + ncol)
        k_rows = c0 + ncol
        y = jnp.dot(u[:, :k_rows], tblk[:k_rows, cols], preferred_element_type=F32)
        y += jnp.dot(pr_b, cblk_r[:, cols], preferred_element_type=F32)
        y += jnp.dot(pi_b, cblk_i[:, cols], preferred_element_type=F32)
        o_ref[0, :, cols] = jax.nn.gelu(y).astype(o_ref.dtype)


def _s5_core(uc, ops, n_batch):
    t_op, bre, bim, cre, cim, a16r, a16i = ops
    j = S5_CHUNK
    gb = S5_GROUP_BLOCK
    nblk, nchunk, width = uc.shape
    lanes = width // j
    hh = lanes // gb
    pp = cre.shape[1]
    def own(width, per_group):
        col_group = jnp.arange(width) // per_group % gb
        m = col_group[None, None, :] == jnp.arange(gb)[:, None, None]
        return jnp.broadcast_to(m, (gb, hh, width)).astype(BF16)

    mask_t = own(j * lanes, hh)
    mask_b = own(gb * pp, pp)
    blk = lambda r, c: pl.BlockSpec((1, r, c), lambda i: (i, 0, 0))
    const = lambda a: pl.BlockSpec(a.shape, lambda i: (0, 0, 0))
    yc = pl.pallas_call(
        functools.partial(_s5_kernel, chunks_per_batch=nchunk // n_batch, n_batch=n_batch),
        grid=(nblk,),
        in_specs=[blk(nchunk, j * lanes), pl.BlockSpec((1, j, hh, lanes), lambda i: (i, 0, 0, 0)),
                  blk(j * hh, gb * pp), blk(j * hh, gb * pp),
                  blk(pp, j * lanes), blk(pp, j * lanes),
                  blk(1, gb * pp), blk(1, gb * pp), const(mask_t), const(mask_b)],
        out_specs=blk(nchunk, j * lanes),
        out_shape=jax.ShapeDtypeStruct((nblk, nchunk, j * lanes), BF16),
        scratch_shapes=[pltpu.VMEM((j * lanes, j * lanes), BF16),
                        pltpu.VMEM((j * lanes, gb * pp), BF16), pltpu.VMEM((j * lanes, gb * pp), BF16),
                        pltpu.VMEM((gb * pp, j * lanes), BF16), pltpu.VMEM((gb * pp, j * lanes), BF16)]
                       + [pltpu.VMEM((nchunk, gb * pp), F32)] * 4,
        compiler_params=_params(("parallel",)),
        name="s5_chunked_scan",
    )(uc, t_op, bre, bim, cre, cim, a16r, a16i, mask_t, mask_b)
    return yc


def _glu_norm_kernel(g_ref, w_ref, gn_ref, o_ref, scr):
    nblk, rows, width = g_ref.shape
    lanes = scr.shape[2]
    steps = width // lanes
    for cb in range(nblk):
        for s in range(steps):
            scr[cb, pl.ds(s, rows, stride=steps), :] = (
                g_ref[cb, :, s * lanes:(s + 1) * lanes].astype(F32))
    g32 = jnp.concatenate([scr[cb] for cb in range(nblk)], axis=1)
    gate = jax.nn.sigmoid(jnp.dot(g32.astype(BF16), w_ref[...], preferred_element_type=F32))
    y = g32 * gate
    y = y * lax.rsqrt(jnp.mean(y * y, axis=-1, keepdims=True) + EPS)
    o_ref[...] = (y * gn_ref[...]).astype(o_ref.dtype)


def _glu_norm(gc, w_glu, layer, gn, steps, tm=512, lanes=128):
    nblk, nrow, width = gc.shape
    t = nrow * steps
    w = nblk * lanes
    return pl.pallas_call(
        _glu_norm_kernel,
        grid=(t // tm,),
        in_specs=[pl.BlockSpec((nblk, tm // steps, width), lambda i: (0, i, 0)),
                  pl.BlockSpec((None, w, w), lambda i: (layer, 0, 0)),
                  pl.BlockSpec((1, w), lambda i: (0, 0))],
        out_specs=pl.BlockSpec((tm, w), lambda i: (i, 0)),
        out_shape=jax.ShapeDtypeStruct((t, w), BF16),
        scratch_shapes=[pltpu.VMEM((nblk, tm, lanes), F32)],
        compiler_params=_params(("parallel",)),
        name="glu_groupnorm",
    )(gc, w_glu, gn.astype(F32).reshape(1, w))


def _head_rms_scale(x, sel_ref, exp_ref, head_dim):
    ss = jnp.dot((x * x).astype(BF16), sel_ref[...], preferred_element_type=F32)
    r = lax.rsqrt(ss * (1.0 / head_dim) + EPS)
    r_hi = r.astype(BF16)
    r_lo = (r - r_hi.astype(F32)).astype(BF16)
    return (jnp.dot(r_hi, exp_ref[...], preferred_element_type=F32)
            + jnp.dot(r_lo, exp_ref[...], preferred_element_type=F32))


def _attn_kernel(sink_ref, q_ref, kp_ref, kc_ref, vp_ref, vc_ref, qg_ref, kg_ref, gn_ref,
                 qsel_ref, qexp_ref, ksel_ref, kexp_ref, o_ref, acc,
                 *, n_q_heads, n_kv_heads, head_dim):
    nblk = pl.program_id(1)
    w = WINDOW
    lw = 2 * head_dim
    grp = n_q_heads // n_kv_heads
    t_loc = lax.broadcasted_iota(jnp.int32, (w, 2 * w), 0)
    s_loc = lax.broadcasted_iota(jnp.int32, (w, 2 * w), 1)
    dist = t_loc + w - s_loc
    valid = (dist >= 0) & (dist < w) & (s_loc + nblk * w >= w)
    dmask = jnp.where(valid, dist.astype(F32), jnp.inf)

    q = q_ref[...].astype(F32)
    qn = (q * _head_rms_scale(q, qsel_ref, qexp_ref, head_dim) * qg_ref[...]).astype(BF16)
    k = jnp.concatenate([kp_ref[...], kc_ref[...]], axis=0).astype(F32)
    kn = k * _head_rms_scale(k, ksel_ref, kexp_ref, head_dim) * kg_ref[...]
    v = jnp.concatenate([vp_ref[...], vc_ref[...]], axis=0).astype(F32)

    lane = lax.broadcasted_iota(jnp.int32, (2 * w, lw), 1)
    lo = lane < head_dim

    def block_diag(x, hk):
        grp_lanes = x[:, (hk // 2) * lw:(hk // 2 + 1) * lw]
        swapped = pltpu.roll(grp_lanes, head_dim, 1)
        low, high = (grp_lanes, swapped) if hk % 2 == 0 else (swapped, grp_lanes)
        return jnp.concatenate([jnp.where(lo, low, 0.0), jnp.where(lo, 0.0, high)],
                               axis=0).astype(BF16)

    lane_q = lax.broadcasted_iota(jnp.int32, (w, lw), 1) < head_dim
    for hk in range(n_kv_heads):
        kb = block_diag(kn, hk)
        vb = block_diag(v, hk)
        for pq in range(grp // 2):
            pair = hk * (grp // 2) + pq
            cols = slice(pair * lw, (pair + 1) * lw)
            s = lax.dot_general(qn[:, cols], kb, (((1,), (1,)), ((), ())),
                                preferred_element_type=F32)
            es, inv = [], []
            for i in range(2):
                hq = 2 * pair + i
                slope = 2.0 ** (-8.0 * (hq + 1) / n_q_heads)
                si = s[:, i * 2 * w:(i + 1) * 2 * w] - slope * dmask
                sink = sink_ref[hq]
                m = jnp.maximum(jnp.max(si, axis=-1, keepdims=True), sink)
                e = jnp.exp(si - m)
                inv.append(1.0 / (jnp.sum(e, axis=-1, keepdims=True) + jnp.exp(sink - m)))
                es.append(e.astype(BF16))
            pv = jnp.dot(jnp.concatenate(es, axis=1), vb, preferred_element_type=F32)
            acc[:, cols] = pv * jnp.where(lane_q, inv[0], inv[1])
    y = acc[...]
    y = y * lax.rsqrt(jnp.mean(y * y, axis=-1, keepdims=True) + EPS)
    o_ref[...] = (y * gn_ref[...]).astype(o_ref.dtype)


def _attention(z, ssm_w, attn_w, kv_w, q_gain, k_gain, sinks, gn, n_batch):
    t = z.shape[0]
    w = WINDOW
    head_dim = q_gain.shape[0]
    n_q = attn_w // head_dim
    n_kv = kv_w // head_dim
    nb = t // n_batch // w
    qcol = ssm_w // attn_w
    kcol = (ssm_w + attn_w) // kv_w
    vcol = kcol + 1
    cur = lambda col: (lambda b, n: (b * nb + n, col))
    prev = lambda col: (lambda b, n: (b * nb + jnp.maximum(n - 1, 0), col))
    full = lambda r, c: pl.BlockSpec((r, c), lambda b, n: (0, 0))
    nsel = 128

    def selectors(width):
        sel = (jnp.arange(width)[:, None] // head_dim == jnp.arange(nsel)[None, :]).astype(BF16)
        return sel, sel.T

    qsel, qexp = selectors(attn_w)
    ksel, kexp = selectors(kv_w)
    qg_row = (jnp.tile(q_gain.astype(F32), n_q) * head_dim ** -0.5).reshape(1, attn_w)
    kg_row = jnp.tile(k_gain.astype(F32), n_kv).reshape(1, kv_w)
    return pl.pallas_call(
        functools.partial(_attn_kernel, n_q_heads=n_q, n_kv_heads=n_kv, head_dim=head_dim),
        grid=(n_batch, nb),
        in_specs=[pl.BlockSpec(memory_space=pltpu.SMEM),
                  pl.BlockSpec((w, attn_w), cur(qcol)),
                  pl.BlockSpec((w, kv_w), prev(kcol)),
                  pl.BlockSpec((w, kv_w), cur(kcol)),
                  pl.BlockSpec((w, kv_w), prev(vcol)),
                  pl.BlockSpec((w, kv_w), cur(vcol)),
                  full(1, attn_w), full(1, kv_w), full(1, attn_w),
                  full(attn_w, nsel), full(nsel, attn_w), full(kv_w, nsel), full(nsel, kv_w)],
        out_specs=pl.BlockSpec((w, attn_w), lambda b, n: (b * nb + n, 0)),
        out_shape=jax.ShapeDtypeStruct((t, attn_w), BF16),
        scratch_shapes=[pltpu.VMEM((w, attn_w), F32)],
        compiler_params=_params(("parallel", "parallel")),
        name="swa_attention",
    )(sinks.astype(F32), z, z, z, z, z, qg_row, kg_row, gn.astype(F32).reshape(1, attn_w),
      qsel, qexp, ksel, kexp)


def _sorting_network(n):
    pairs = []
    p = 1
    while p < n:
        k = p
        while k >= 1:
            for j in range(k % p, n - k, 2 * k):
                for i in range(min(k, n - j - k)):
                    if (i + j) // (2 * p) == (i + j + k) // (2 * p):
                        pairs.append((i + j, i + j + k))
            k //= 2
        p *= 2
    return pairs


def _kth_largest_rows(vals, k):
    sub = 8
    tiles = [vals[r:r + sub, :] for r in range(0, vals.shape[0], sub)]
    for i, j in _sorting_network(len(tiles)):
        hi = jnp.maximum(tiles[i], tiles[j])
        tiles[j] = jnp.minimum(tiles[i], tiles[j])
        tiles[i] = hi
    out = []
    for it in range(k):
        m = jnp.max(tiles[0], axis=0, keepdims=True)
        out.append(m)
        hit = tiles[0] == m
        depth = min(len(tiles), k - it)
        for r in range(depth - 1):
            tiles[r] = jnp.where(hit, tiles[r + 1], tiles[r])
        if depth == len(tiles):
            tiles[depth - 1] = jnp.where(hit, -jnp.inf, tiles[depth - 1])
    return out


def _kth_largest_value(vals, k):
    work = vals
    left = jnp.full((1, vals.shape[1]), float(k), F32)
    kth = jnp.full((1, vals.shape[1]), -jnp.inf, F32)
    for _ in range(k):
        m = jnp.max(work, axis=0, keepdims=True)
        hit = work == m
        kth = jnp.where(left > 0.0, m, kth)
        left = left - jnp.sum(jnp.where(hit, 1.0, 0.0), axis=0, keepdims=True)
        work = jnp.where(hit, -jnp.inf, work)
    return kth


def _staircase_candidates(v1, v2):
    k = PEER_TOPK + 1
    sub = 8
    v1m = jnp.concatenate(v1[:2 * sub], axis=0)
    v2m = jnp.concatenate(v2[:2 * sub], axis=0)
    row = lax.broadcasted_iota(jnp.int32, (sub, v2m.shape[1]), 0)
    blocks = [v1[0] + v2m]
    for i in range(1, sub):
        keep = k // (i + 1)
        blk = v1[i] + v2m[0:sub]
        blocks.append(blk if keep >= sub else jnp.where(row < keep, blk, -jnp.inf))
    blocks.append(v1m[sub:] + v2[0])
    pad = jnp.full((sub - 2, v2m.shape[1]), -jnp.inf, F32)
    blocks.append(jnp.concatenate([v1[2 * sub] + v2[0], v1[0] + v2[2 * sub], pad], axis=0))
    return jnp.concatenate(blocks, axis=0)


def _peer_topk_kernel(qt_ref, k1_ref, k2_ref, s1_ref, s2_ref, st_ref, *, n_heads, n_keys):
    half = k1_ref.shape[1]
    hp = lax.Precision.HIGHEST
    k1 = k1_ref[...]
    k2 = k2_ref[...]
    taus = []
    for h in range(n_heads):
        base = h * 2 * half
        s1 = jnp.dot(k1, qt_ref[base:base + half, :], preferred_element_type=F32, precision=hp)
        s2 = jnp.dot(k2, qt_ref[base + half:base + 2 * half, :], preferred_element_type=F32,
                     precision=hp)
        a1 = s1 * LOG2E
        a2 = s2 * LOG2E
        v1 = _kth_largest_rows(a1, PEER_TOPK + 1)
        v2 = _kth_largest_rows(a2, PEER_TOPK + 1)
        v1s = [v - v1[0] for v in v1]
        v2s = [v - v2[0] for v in v2]
        cand = _staircase_candidates(v1s, v2s)
        picked = cand >= _kth_largest_value(cand, PEER_TOPK)
        zsum = jnp.sum(jnp.where(picked, jnp.exp2(cand), 0.0), axis=0, keepdims=True)
        shift1 = v1[0] + jnp.log2(zsum)
        cand_z = _staircase_candidates([v - shift1 for v in v1], v2s)
        last_in = jnp.min(jnp.where(picked, cand_z, jnp.inf), axis=0, keepdims=True)
        first_out = jnp.max(jnp.where(picked, -jnp.inf, cand_z), axis=0, keepdims=True)
        taus.append(0.5 * (last_in + first_out))
        s1_ref[h * n_keys:(h + 1) * n_keys, :] = a1 - shift1
        s2_ref[h * n_keys:(h + 1) * n_keys, :] = a2 - v2[0]
    st_ref[...] = jnp.concatenate(taus, axis=0)


def _peer_topk(qt, k1, k2, tn=256):
    hq, t = qt.shape
    n_keys, half = k1.shape
    n_heads = hq // (2 * half)
    srows = n_heads * n_keys
    col = lambda rows: pl.BlockSpec((rows, tn), lambda i: (0, i))
    return pl.pallas_call(
        functools.partial(_peer_topk_kernel, n_heads=n_heads, n_keys=n_keys),
        grid=(t // tn,),
        in_specs=[col(hq),
                  pl.BlockSpec((n_keys, half), lambda i: (0, 0)),
                  pl.BlockSpec((n_keys, half), lambda i: (0, 0))],
        out_specs=[col(srows), col(srows), col(n_heads)],
        out_shape=[jax.ShapeDtypeStruct((srows, t), F32), jax.ShapeDtypeStruct((srows, t), F32),
                   jax.ShapeDtypeStruct((n_heads, t), F32)],
        compiler_params=_params(("parallel",)),
        name="peer_topk",
    )(qt, k1.astype(F32), k2.astype(F32))


def _peer_dense_kernel(h_ref, u_ref, v_ref, l1_ref, l2_ref, tau_ref, o_ref, tau8, l1b, wt,
                       *, n_heads, n_keys):
    j = pl.program_id(1)
    te = v_ref.shape[0]
    tm = h_ref.shape[0]
    d = v_ref.shape[1]
    na = te // n_keys
    sub = 8
    dc = 1024

    @pl.when(j == 0)
    def _():
        o_ref[...] = jnp.zeros_like(o_ref)
        for h in range(n_heads):
            tau8[h * sub:(h + 1) * sub, :] = jnp.broadcast_to(tau_ref[h:h + 1, :], (sub, tm))

    for aa in range(na):
        for h in range(n_heads):
            row = l1_ref[pl.ds(h * n_keys + j * na + aa, 1), :]
            k = (aa * n_heads + h) * sub
            l1b[k:k + sub, :] = jnp.broadcast_to(row, (sub, tm))

    ax = 2
    link = jnp.zeros((sub, tm), F32)
    for a0 in range(0, na, ax):
        for b0 in range(0, n_keys, 2 * sub):
            acc = [[link, link] for _ in range(ax)]
            for h in range(n_heads):
                t8 = tau8[h * sub:(h + 1) * sub, :]
                r2 = [slice(h * n_keys + b0 + y * sub, h * n_keys + b0 + (y + 1) * sub) for y in range(2)]
                l2v = [l2_ref[r, :] for r in r2]
                for x in range(ax):
                    k = ((a0 + x) * n_heads + h) * sub
                    l1v = l1b[k:k + sub, :]
                    for y in range(2):
                        logit = l1v + l2v[y]
                        acc[x][y] = acc[x][y] + jnp.exp2(jnp.where(logit >= t8, logit, -jnp.inf))
            for x in range(ax):
                r0 = (a0 + x) * n_keys + b0
                wt[r0:r0 + 2 * sub, :] = jnp.concatenate(acc[x], axis=0).astype(wt.dtype)
            link = jnp.minimum(pltpu.roll(acc[0][0], 1, 1), 0.0)

    w = wt[...].T
    act = lax.dot_general(h_ref[...], u_ref[...], (((1,), (1,)), ((), ())),
                          preferred_element_type=F32)
    g = jax.nn.gelu(act).astype(BF16) * w
    for c0 in range(0, d, dc):
        o_ref[:, c0:c0 + dc] += jnp.dot(g, v_ref[:, c0:c0 + dc], preferred_element_type=F32)


def _peer_dense(h2, u_tab, v_tab, layer, s1t, s2t, stats, tm=512, te=512):
    t, d = h2.shape
    ne = v_tab.shape[1]
    n_keys = int(round(math.sqrt(ne)))
    n_heads = s1t.shape[0] // n_keys
    srows = s1t.shape[0]
    na = te // n_keys
    tok = lambda rows: pl.BlockSpec((rows, tm), lambda i, j: (0, i))
    return pl.pallas_call(
        functools.partial(_peer_dense_kernel, n_heads=n_heads, n_keys=n_keys),
        grid=(t // tm, ne // te),
        in_specs=[pl.BlockSpec((tm, d), lambda i, j: (i, 0)),
                  pl.BlockSpec((None, te, d), lambda i, j: (layer, j, 0)),
                  pl.BlockSpec((None, te, d), lambda i, j: (layer, j, 0)),
                  tok(srows), tok(srows), tok(n_heads)],
        out_specs=pl.BlockSpec((tm, d), lambda i, j: (i, 0)),
        out_shape=jax.ShapeDtypeStruct((t, d), F32),
        scratch_shapes=[pltpu.VMEM((n_heads * 8, tm), F32),
                        pltpu.VMEM((na * n_heads * 8, tm), F32),
                        pltpu.VMEM((te, tm), BF16)],
        compiler_params=_params(("parallel", "arbitrary")),
        name="peer_dense",
    )(h2, u_tab, v_tab, s1t, s2t, stats)


def kernel(x, c, w_ada, b_ada, ada_layer, norm1_g, norm2_g, w_in, lam_re, lam_im, log_dt, b_re, b_im,
           c_re, c_im, d_skip, w_glu, q_gain, k_gain, sinks, gn_ssm, gn_attn, w_out, peer_wq, peer_k1,
           peer_k2, peer_u, peer_v):
    bsz, seq, d = x.shape
    t = bsz * seq
    depth = w_in.shape[0]
    n_mod = ada_layer.shape[1]
    ssm_w = w_glu.shape[1]
    attn_w = gn_attn.shape[1]
    kv_w = (w_in.shape[2] - ssm_w - attn_w) // 2

    w_in_b, w_glu_b, w_out_b = w_in, w_glu.astype(BF16), w_out
    wq_t_b = peer_wq.transpose(0, 2, 1).astype(BF16)
    u_b, v_b = peer_u.astype(BF16), peer_v.astype(BF16)

    cond = _cond(c, w_ada, b_ada).reshape(bsz, n_mod, d)
    xf = x.astype(F32).reshape(t, d)
    delta, gate_prev = None, None
    for l in range(depth):
        mod = cond + ada_layer[l].astype(F32)
        shift1, scale1, gate1, shift2, scale2, gate2 = (mod[:, i] for i in range(n_mod))

        xf, h = _adaln_norm(xf, delta, gate_prev, norm1_g[l], scale1, shift1, seq)
        z_ssm = _matmul_channel_blocks(h, w_in_b, l, ssm_w, BF16, tm=1024, tn=512, steps=S5_CHUNK)
        z_attn = _matmul(h, w_in_b, l, ssm_w, attn_w + 2 * kv_w, BF16, tm=1024, tn=512)
        ops = _s5_operators(lam_re[l], lam_im[l], log_dt[l], b_re[l], b_im[l], c_re[l], c_im[l],
                            d_skip[l])
        g = _s5_core(z_ssm, ops, bsz)
        y_ssm = _glu_norm(g, w_glu_b, l, gn_ssm[l], steps=S5_CHUNK)
        y_attn = _attention(z_attn, 0, attn_w, kv_w, q_gain[l], k_gain[l], sinks[l], gn_attn[l], bsz)
        mixed = _matmul_concat(y_ssm, y_attn, w_out_b, l, BF16, tm=1024, tn=512)

        xf, h2 = _adaln_norm(xf, mixed, gate1, norm2_g[l], scale2, shift2, seq)
        qt = _matmul_nt(wq_t_b, l, h2, F32, tn=512)
        s1t, s2t, stats = _peer_topk(qt, peer_k1[l], peer_k2[l])
        delta = _peer_dense(h2, u_b, v_b, l, s1t, s2t, stats)
        gate_prev = gate2
    out = _residual_add(xf, delta, gate_prev, seq)
    return out.reshape(bsz, seq, d).astype(x.dtype)
```

```python
import functools
import math

import jax
import jax.numpy as jnp
from jax import lax
from jax.experimental import pallas as pl
from jax.experimental.pallas import tpu as pltpu

F32 = jnp.float32
BF16 = jnp.bfloat16
EPS = 1e-6
WINDOW = 128
PEER_TOPK = 16
LOG2E = 1.4426950408889634
GELU_C0 = math.sqrt(2.0 / math.pi)
GELU_C1 = 0.044715 * GELU_C0
S5_CHUNK = 16
S5_GROUP_BLOCK = 8
V7X_VMEM_LIMIT = 56 * 1024 * 1024


def _params(semantics, vmem=V7X_VMEM_LIMIT, flags=None):
    return pltpu.CompilerParams(dimension_semantics=semantics, vmem_limit_bytes=vmem, flags=flags)


def _mm_kernel(a_ref, b_ref, o_ref):
    o_ref[...] = jnp.dot(a_ref[...], b_ref[...].astype(a_ref.dtype),
                         preferred_element_type=F32).astype(o_ref.dtype)


def _matmul(a, b, layer, col0, n, out_dtype, tm, tn):
    m, k = a.shape
    j0 = col0 // tn
    return pl.pallas_call(
        _mm_kernel,
        grid=(m // tm, n // tn),
        in_specs=[pl.BlockSpec((tm, k), lambda i, j: (i, 0)),
                  pl.BlockSpec((None, k, tn), lambda i, j: (layer, 0, j0 + j))],
        out_specs=pl.BlockSpec((tm, tn), lambda i, j: (i, j)),
        out_shape=jax.ShapeDtypeStruct((m, n), out_dtype),
        compiler_params=_params(("parallel", "parallel")),
        name="matmul",
    )(a, b)


def _mm_cb_kernel(a_ref, b_ref, o_ref, scr):
    res = jnp.dot(a_ref[...], b_ref[...].astype(a_ref.dtype), preferred_element_type=F32)
    nb, rows, width = o_ref.shape
    lanes = scr.shape[2]
    steps = width // lanes
    for k in range(nb):
        scr[k] = res[:, k * lanes:(k + 1) * lanes]
    for k in range(nb):
        for s in range(steps):
            o_ref[k, :, s * lanes:(s + 1) * lanes] = (
                scr[k, pl.ds(s, rows, stride=steps), :].astype(o_ref.dtype))


def _matmul_channel_blocks(a, b, layer, n, out_dtype, tm, tn, steps, lanes=128):
    m, k = a.shape
    return pl.pallas_call(
        _mm_cb_kernel,
        grid=(m // tm, n // tn),
        in_specs=[pl.BlockSpec((tm, k), lambda i, j: (i, 0)),
                  pl.BlockSpec((None, k, tn), lambda i, j: (layer, 0, j))],
        out_specs=pl.BlockSpec((tn // lanes, tm // steps, steps * lanes), lambda i, j: (j, i, 0)),
        out_shape=jax.ShapeDtypeStruct((n // lanes, m // steps, steps * lanes), out_dtype),
        scratch_shapes=[pltpu.VMEM((tn // lanes, tm, lanes), F32)],
        compiler_params=_params(("parallel", "parallel")),
        name="matmul_channel_blocks",
    )(a, b)


def _mm2_kernel(a1_ref, a2_ref, b1_ref, b2_ref, o_ref):
    acc = jnp.dot(a1_ref[...], b1_ref[...].astype(a1_ref.dtype), preferred_element_type=F32)
    acc += jnp.dot(a2_ref[...], b2_ref[...].astype(a2_ref.dtype), preferred_element_type=F32)
    o_ref[...] = acc.astype(o_ref.dtype)


def _matmul_concat(a1, a2, b, layer, out_dtype, tm, tn):
    m, k1 = a1.shape
    k2 = a2.shape[1]
    assert k1 == k2
    n = b.shape[2]
    return pl.pallas_call(
        _mm2_kernel,
        grid=(m // tm, n // tn),
        in_specs=[pl.BlockSpec((tm, k1), lambda i, j: (i, 0)),
                  pl.BlockSpec((tm, k2), lambda i, j: (i, 0)),
                  pl.BlockSpec((None, k1, tn), lambda i, j: (layer, 0, j)),
                  pl.BlockSpec((None, k2, tn), lambda i, j: (layer, 1, j))],
        out_specs=pl.BlockSpec((tm, tn), lambda i, j: (i, j)),
        out_shape=jax.ShapeDtypeStruct((m, n), out_dtype),
        compiler_params=_params(("parallel", "parallel")),
        name="matmul_concat",
    )(a1, a2, b, b)


def _mm_nt_kernel(a_ref, b_ref, o_ref):
    o_ref[...] = lax.dot_general(a_ref[...], b_ref[...], (((1,), (1,)), ((), ())),
                                 preferred_element_type=F32).astype(o_ref.dtype)


def _matmul_nt(a, layer, b, out_dtype, tn):
    _, m, k = a.shape
    n = b.shape[0]
    return pl.pallas_call(
        _mm_nt_kernel,
        grid=(n // tn,),
        in_specs=[pl.BlockSpec((None, m, k), lambda j: (layer, 0, 0)),
                  pl.BlockSpec((tn, k), lambda j: (j, 0))],
        out_specs=pl.BlockSpec((m, tn), lambda j: (0, j)),
        out_shape=jax.ShapeDtypeStruct((m, n), out_dtype),
        compiler_params=_params(("parallel",)),
        name="matmul_nt",
    )(a, b)


def _cond_kernel(c_ref, w_ref, b_ref, o_ref):
    c = c_ref[...]
    s = c * jax.nn.sigmoid(c)
    o_ref[...] = jnp.dot(s, w_ref[...], preferred_element_type=F32,
                         precision=lax.Precision.HIGHEST) + b_ref[...]


def _cond(c, w_ada, b_ada, tn=512):
    bsz, d = c.shape
    n = w_ada.shape[1]
    rows = 8
    cp = jnp.zeros((rows, d), F32).at[:bsz].set(c.astype(F32))
    out = pl.pallas_call(
        _cond_kernel,
        grid=(n // tn,),
        in_specs=[pl.BlockSpec((rows, d), lambda j: (0, 0)),
                  pl.BlockSpec((d, tn), lambda j: (0, j)),
                  pl.BlockSpec((1, tn), lambda j: (0, j))],
        out_specs=pl.BlockSpec((rows, tn), lambda j: (0, j)),
        out_shape=jax.ShapeDtypeStruct((rows, n), F32),
        compiler_params=_params(("parallel",)),
        name="adaln_cond",
    )(cp, w_ada.astype(F32), b_ada.astype(F32).reshape(1, n))
    return out[:bsz]


def _norm_body(x, g_ref, scale_ref, shift_ref, h_ref):
    y = x * lax.rsqrt(jnp.mean(x * x, axis=-1, keepdims=True) + EPS)
    y = y * g_ref[...]
    h_ref[...] = (y * (1.0 + scale_ref[0]) + shift_ref[0]).astype(h_ref.dtype)


def _norm_kernel(x_ref, g_ref, scale_ref, shift_ref, h_ref):
    _norm_body(x_ref[...], g_ref, scale_ref, shift_ref, h_ref)


def _resnorm_kernel(x_ref, d_ref, gate_ref, g_ref, scale_ref, shift_ref, xo_ref, h_ref):
    x = x_ref[...] + gate_ref[0] * d_ref[...]
    xo_ref[...] = x
    _norm_body(x, g_ref, scale_ref, shift_ref, h_ref)


def _adaln_norm(x, delta, gate, g, scale, shift, rows_per_batch, tm=256):
    t, d = x.shape
    tpb = rows_per_batch // tm
    row = pl.BlockSpec((tm, d), lambda i: (i, 0))
    per_batch = pl.BlockSpec((1, 1, d), lambda i: (i // tpb, 0, 0))
    vec = pl.BlockSpec((1, d), lambda i: (0, 0))
    g2 = g.astype(F32).reshape(1, d)
    b3 = lambda v: v.astype(F32).reshape(v.shape[0], 1, d)
    if delta is None:
        h = pl.pallas_call(
            _norm_kernel,
            grid=(t // tm,),
            in_specs=[row, vec, per_batch, per_batch],
            out_specs=row,
            out_shape=jax.ShapeDtypeStruct((t, d), BF16),
            compiler_params=_params(("parallel",)),
            name="adaln_norm",
        )(x, g2, b3(scale), b3(shift))
        return x, h
    return pl.pallas_call(
        _resnorm_kernel,
        grid=(t // tm,),
        in_specs=[row, row, per_batch, vec, per_batch, per_batch],
        out_specs=[row, row],
        out_shape=[jax.ShapeDtypeStruct((t, d), F32), jax.ShapeDtypeStruct((t, d), BF16)],
        compiler_params=_params(("parallel",)),
        name="residual_adaln_norm",
    )(x, delta, b3(gate), g2, b3(scale), b3(shift))


def _residual_kernel(x_ref, d_ref, gate_ref, o_ref):
    o_ref[...] = x_ref[...] + gate_ref[0] * d_ref[...]


def _residual_add(x, delta, gate, rows_per_batch, tm=256):
    t, d = x.shape
    tpb = rows_per_batch // tm
    row = pl.BlockSpec((tm, d), lambda i: (i, 0))
    return pl.pallas_call(
        _residual_kernel,
        grid=(t // tm,),
        in_specs=[row, row, pl.BlockSpec((1, 1, d), lambda i: (i // tpb, 0, 0))],
        out_specs=row,
        out_shape=jax.ShapeDtypeStruct((t, d), F32),
        compiler_params=_params(("parallel",)),
        name="residual_add",
    )(x, delta, gate.astype(F32).reshape(gate.shape[0], 1, d))


def _s5_operators(lam_re, lam_im, log_dt, b_re, b_im, c_re, c_im, d_skip):
    hp = lax.Precision.HIGHEST
    j = S5_CHUNK
    f = lambda v: v.astype(F32)
    lr, li = f(lam_re), f(lam_im)
    g, p = lr.shape
    h = b_re.shape[-1]
    dt = jnp.exp(f(log_dt))[:, None]
    mag = jnp.exp(lr * dt)
    ar = mag * jnp.cos(li * dt)
    ai = mag * jnp.sin(li * dt)
    den = lr * lr + li * li
    kr = ((ar - 1.0) * lr + ai * li) / den
    ki = (ai * lr - (ar - 1.0) * li) / den
    br, bi = f(b_re), f(b_im)
    bbr = kr[..., None] * br - ki[..., None] * bi
    bbi = kr[..., None] * bi + ki[..., None] * br
    cr, ci = f(c_re), f(c_im)
    pr, pi = [jnp.ones_like(ar)], [jnp.zeros_like(ar)]
    for _ in range(j):
        pr.append(pr[-1] * ar - pi[-1] * ai)
        pi.append(pr[-2] * ai + pi[-1] * ar)
    pr, pi = jnp.stack(pr), jnp.stack(pi)
    er = pr[:j, :, :, None] * bbr - pi[:j, :, :, None] * bbi
    ei = pr[:j, :, :, None] * bbi + pi[:j, :, :, None] * bbr
    kk = (jnp.einsum('ghp,ngpk->nghk', cr, er, precision=hp)
          - jnp.einsum('ghp,ngpk->nghk', ci, ei, precision=hp))
    kk = kk.at[0].add(f(d_skip)[:, :, None] * jnp.eye(h, dtype=F32))
    gb = S5_GROUP_BLOCK
    nblk = g // gb
    t_op = kk.reshape(j, nblk, gb, h, h).transpose(1, 0, 4, 2, 3).reshape(nblk, j, h, gb * h)

    def chunk_in(e):
        return (e[::-1].reshape(j, nblk, gb, p, h).transpose(1, 0, 4, 2, 3)
                .reshape(nblk, j * h, gb * p))

    p1r, p1i = pr[1:], pi[1:]
    cc_r = cr[None] * p1r[:, :, None, :] - ci[None] * p1i[:, :, None, :]
    cc_i = -(cr[None] * p1i[:, :, None, :] + ci[None] * p1r[:, :, None, :])

    def chunk_out(m):
        return (m.reshape(j, nblk, gb, h, p).transpose(1, 4, 0, 2, 3)
                .reshape(nblk, p, j * gb * h))

    a16r = pr[j].reshape(nblk, 1, gb * p)
    a16i = pi[j].reshape(nblk, 1, gb * p)
    return (t_op.astype(BF16), chunk_in(er).astype(BF16), chunk_in(ei).astype(BF16),
            chunk_out(cc_r).astype(BF16), chunk_out(cc_i).astype(BF16), a16r, a16i)


def _s5_kernel(u_ref, t_ref, bre_ref, bim_ref, cre_ref, cim_ref, ar_ref, ai_ref, mt_ref, mb_ref,
               o_ref, tblk, bblk_r, bblk_i, cblk_r, cblk_i, zre, zim, pre, pim,
               *, chunks_per_batch, n_batch):
    gb = S5_GROUP_BLOCK
    j = S5_CHUNK
    hh = t_ref.shape[2]
    pp = cre_ref.shape[1]
    lanes = gb * hh
    ncol = 4 * lanes
    for i in range(j):
        j_first = (i * lanes // ncol) * ncol // lanes
        for gl in range(gb):
            src = slice(i * hh, (i + 1) * hh)
            dst = slice(i * lanes + gl * hh, i * lanes + (gl + 1) * hh)
            m_gl = mt_ref[gl, :, 0:lanes]
            for jj in range(j_first, j):
                blk = t_ref[0, jj - i] * m_gl if jj >= i else jnp.zeros((hh, lanes), tblk.dtype)
                tblk[dst, jj * lanes:(jj + 1) * lanes] = blk
            bblk_r[dst, :] = bre_ref[0, src, :] * mb_ref[gl]
            bblk_i[dst, :] = bim_ref[0, src, :] * mb_ref[gl]
    for gl in range(gb):
        for r0 in range(0, pp, hh):
            src = slice(r0, r0 + hh)
            dst = slice(gl * pp + r0, gl * pp + r0 + hh)
            cblk_r[dst, :] = cre_ref[0, src, :] * mt_ref[gl]
            cblk_i[dst, :] = cim_ref[0, src, :] * mt_ref[gl]

    u = u_ref[0]
    zre[...] = jnp.dot(u, bblk_r[...], preferred_element_type=F32)
    zim[...] = jnp.dot(u, bblk_i[...], preferred_element_type=F32)
    ar = ar_ref[0]
    ai = ai_ref[0]

    def step(c, carry):
        new = []
        for b in range(n_batch):
            sr, si = carry[2 * b], carry[2 * b + 1]
            row = pl.ds(b * chunks_per_batch + c, 1)
            pre[row, :] = sr
            pim[row, :] = si
            new.append(ar * sr - ai * si + zre[row, :])
            new.append(ar * si + ai * sr + zim[row, :])
        return tuple(new)

    zero = jnp.zeros((1, zre.shape[1]), F32)
    lax.fori_loop(0, chunks_per_batch, step, (zero,) * (2 * n_batch), unroll=8)

    pr_b = pre[...].astype(BF16)
    pi_b = pim[...].astype(BF16)
    for c0 in range(0, j * lanes, ncol):
        cols = slice(c0, c0 + ncol)
        k_rows = c0 + ncol
        y = jnp.dot(u[:, :k_rows], tblk[:k_rows, cols], preferred_element_type=F32)
        y += jnp.dot(pr_b, cblk_r[:, cols], preferred_element_type=F32)
        y += jnp.dot(pi_b, cblk_i[:, cols], preferred_element_type=F32)
        o_ref[0, :, cols] = jax.nn.gelu(y).astype(o_ref.dtype)


def _s5_core(uc, ops, n_batch):
    t_op, bre, bim, cre, cim, a16r, a16i = ops
    j = S5_CHUNK
    gb = S5_GROUP_BLOCK
    nblk, nchunk, width = uc.shape
    lanes = width // j
    hh = lanes // gb
    pp = cre.shape[1]
    def own(width, per_group):
        col_group = jnp.arange(width) // per_group % gb
        m = col_group[None, None, :] == jnp.arange(gb)[:, None, None]
        return jnp.broadcast_to(m, (gb, hh, width)).astype(BF16)

    mask_t = own(j * lanes, hh)
    mask_b = own(gb * pp, pp)
    blk = lambda r, c: pl.BlockSpec((1, r, c), lambda i: (i, 0, 0))
    const = lambda a: pl.BlockSpec(a.shape, lambda i: (0, 0, 0))
    yc = pl.pallas_call(
        functools.partial(_s5_kernel, chunks_per_batch=nchunk // n_batch, n_batch=n_batch),
        grid=(nblk,),
        in_specs=[blk(nchunk, j * lanes), pl.BlockSpec((1, j, hh, lanes), lambda i: (i, 0, 0, 0)),
                  blk(j * hh, gb * pp), blk(j * hh, gb * pp),
                  blk(pp, j * lanes), blk(pp, j * lanes),
                  blk(1, gb * pp), blk(1, gb * pp), const(mask_t), const(mask_b)],
        out_specs=blk(nchunk, j * lanes),
        out_shape=jax.ShapeDtypeStruct((nblk, nchunk, j * lanes), BF16),
        scratch_shapes=[pltpu.VMEM((j * lanes, j * lanes), BF16),
                        pltpu.VMEM((j * lanes, gb * pp), BF16), pltpu.VMEM((j * lanes, gb * pp), BF16),
                        pltpu.VMEM((gb * pp, j * lanes), BF16), pltpu.VMEM((gb * pp, j * lanes), BF16)]
                       + [pltpu.VMEM((nchunk, gb * pp), F32)] * 4,
        compiler_params=_params(("parallel",)),
        name="s5_chunked_scan",
    )(uc, t_op, bre, bim, cre, cim, a16r, a16i, mask_t, mask_b)
    return yc


def _glu_norm_kernel(g_ref, w_ref, gn_ref, o_ref, scr):
    nblk, rows, width = g_ref.shape
    lanes = scr.shape[2]
    steps = width // lanes
    for cb in range(nblk):
        for s in range(steps):
            scr[cb, pl.ds(s, rows, stride=steps), :] = (
                g_ref[cb, :, s * lanes:(s + 1) * lanes].astype(F32))
    g32 = jnp.concatenate([scr[cb] for cb in range(nblk)], axis=1)
    gate = jax.nn.sigmoid(jnp.dot(g32.astype(BF16), w_ref[...], preferred_element_type=F32))
    y = g32 * gate
    y = y * lax.rsqrt(jnp.mean(y * y, axis=-1, keepdims=True) + EPS)
    o_ref[...] = (y * gn_ref[...]).astype(o_ref.dtype)


def _glu_norm(gc, w_glu, layer, gn, steps, tm=512, lanes=128):
    nblk, nrow, width = gc.shape
    t = nrow * steps
    w = nblk * lanes
    return pl.pallas_call(
        _glu_norm_kernel,
        grid=(t // tm,),
        in_specs=[pl.BlockSpec((nblk, tm // steps, width), lambda i: (0, i, 0)),
                  pl.BlockSpec((None, w, w), lambda i: (layer, 0, 0)),
                  pl.BlockSpec((1, w), lambda i: (0, 0))],
        out_specs=pl.BlockSpec((tm, w), lambda i: (i, 0)),
        out_shape=jax.ShapeDtypeStruct((t, w), BF16),
        scratch_shapes=[pltpu.VMEM((nblk, tm, lanes), F32)],
        compiler_params=_params(("parallel",)),
        name="glu_groupnorm",
    )(gc, w_glu, gn.astype(F32).reshape(1, w))


def _head_rms_scale(x, sel_ref, exp_ref, head_dim):
    ss = jnp.dot((x * x).astype(BF16), sel_ref[...], preferred_element_type=F32)
    r = lax.rsqrt(ss * (1.0 / head_dim) + EPS)
    r_hi = r.astype(BF16)
    r_lo = (r - r_hi.astype(F32)).astype(BF16)
    return (jnp.dot(r_hi, exp_ref[...], preferred_element_type=F32)
            + jnp.dot(r_lo, exp_ref[...], preferred_element_type=F32))


def _attn_kernel(sink_ref, q_ref, kp_ref, kc_ref, vp_ref, vc_ref, qg_ref, kg_ref, gn_ref,
                 qsel_ref, qexp_ref, ksel_ref, kexp_ref, o_ref, acc,
                 *, n_q_heads, n_kv_heads, head_dim):
    nblk = pl.program_id(1)
    w = WINDOW
    lw = 2 * head_dim
    grp = n_q_heads // n_kv_heads
    t_loc = lax.broadcasted_iota(jnp.int32, (w, 2 * w), 0)
    s_loc = lax.broadcasted_iota(jnp.int32, (w, 2 * w), 1)
    dist = t_loc + w - s_loc
    valid = (dist >= 0) & (dist < w) & (s_loc + nblk * w >= w)
    dmask = jnp.where(valid, dist.astype(F32), jnp.inf)

    q = q_ref[...].astype(F32)
    qn = (q * _head_rms_scale(q, qsel_ref, qexp_ref, head_dim) * qg_ref[...]).astype(BF16)
    k = jnp.concatenate([kp_ref[...], kc_ref[...]], axis=0).astype(F32)
    kn = k * _head_rms_scale(k, ksel_ref, kexp_ref, head_dim) * kg_ref[...]
    v = jnp.concatenate([vp_ref[...], vc_ref[...]], axis=0).astype(F32)

    lane = lax.broadcasted_iota(jnp.int32, (2 * w, lw), 1)
    lo = lane < head_dim

    def block_diag(x, hk):
        grp_lanes = x[:, (hk // 2) * lw:(hk // 2 + 1) * lw]
        swapped = pltpu.roll(grp_lanes, head_dim, 1)
        low, high = (grp_lanes, swapped) if hk % 2 == 0 else (swapped, grp_lanes)
        return jnp.concatenate([jnp.where(lo, low, 0.0), jnp.where(lo, 0.0, high)],
                               axis=0).astype(BF16)

    lane_q = lax.broadcasted_iota(jnp.int32, (w, lw), 1) < head_dim
    for hk in range(n_kv_heads):
        kb = block_diag(kn, hk)
        vb = block_diag(v, hk)
        for pq in range(grp // 2):
            pair = hk * (grp // 2) + pq
            cols = slice(pair * lw, (pair + 1) * lw)
            s = lax.dot_general(qn[:, cols], kb, (((1,), (1,)), ((), ())),
                                preferred_element_type=F32)
            es, inv = [], []
            for i in range(2):
                hq = 2 * pair + i
                slope = 2.0 ** (-8.0 * (hq + 1) / n_q_heads)
                si = s[:, i * 2 * w:(i + 1) * 2 * w] - slope * dmask
                sink = sink_ref[hq]
                m = jnp.maximum(jnp.max(si, axis=-1, keepdims=True), sink)
                e = jnp.exp(si - m)
                inv.append(1.0 / (jnp.sum(e, axis=-1, keepdims=True) + jnp.exp(sink - m)))
                es.append(e.astype(BF16))
            pv = jnp.dot(jnp.concatenate(es, axis=1), vb, preferred_element_type=F32)
            acc[:, cols] = pv * jnp.where(lane_q, inv[0], inv[1])
    y = acc[...]
    y = y * lax.rsqrt(jnp.mean(y * y, axis=-1, keepdims=True) + EPS)
    o_ref[...] = (y * gn_ref[...]).astype(o_ref.dtype)


def _attention(z, ssm_w, attn_w, kv_w, q_gain, k_gain, sinks, gn, n_batch):
    t = z.shape[0]
    w = WINDOW
    head_dim = q_gain.shape[0]
    n_q = attn_w // head_dim
    n_kv = kv_w // head_dim
    nb = t // n_batch // w
    qcol = ssm_w // attn_w
    kcol = (ssm_w + attn_w) // kv_w
    vcol = kcol + 1
    cur = lambda col: (lambda b, n: (b * nb + n, col))
    prev = lambda col: (lambda b, n: (b * nb + jnp.maximum(n - 1, 0), col))
    full = lambda r, c: pl.BlockSpec((r, c), lambda b, n: (0, 0))
    nsel = 128

    def selectors(width):
        sel = (jnp.arange(width)[:, None] // head_dim == jnp.arange(nsel)[None, :]).astype(BF16)
        return sel, sel.T

    qsel, qexp = selectors(attn_w)
    ksel, kexp = selectors(kv_w)
    qg_row = (jnp.tile(q_gain.astype(F32), n_q) * head_dim ** -0.5).reshape(1, attn_w)
    kg_row = jnp.tile(k_gain.astype(F32), n_kv).reshape(1, kv_w)
    return pl.pallas_call(
        functools.partial(_attn_kernel, n_q_heads=n_q, n_kv_heads=n_kv, head_dim=head_dim),
        grid=(n_batch, nb),
        in_specs=[pl.BlockSpec(memory_space=pltpu.SMEM),
                  pl.BlockSpec((w, attn_w), cur(qcol)),
                  pl.BlockSpec((w, kv_w), prev(kcol)),
                  pl.BlockSpec((w, kv_w), cur(kcol)),
                  pl.BlockSpec((w, kv_w), prev(vcol)),
                  pl.BlockSpec((w, kv_w), cur(vcol)),
                  full(1, attn_w), full(1, kv_w), full(1, attn_w),
                  full(attn_w, nsel), full(nsel, attn_w), full(kv_w, nsel), full(nsel, kv_w)],
        out_specs=pl.BlockSpec((w, attn_w), lambda b, n: (b * nb + n, 0)),
        out_shape=jax.ShapeDtypeStruct((t, attn_w), BF16),
        scratch_shapes=[pltpu.VMEM((w, attn_w), F32)],
        compiler_params=_params(("parallel", "parallel")),
        name="swa_attention",
    )(sinks.astype(F32), z, z, z, z, z, qg_row, kg_row, gn.astype(F32).reshape(1, attn_w),
      qsel, qexp, ksel, kexp)


def _sorting_network(n):
    pairs = []
    p = 1
    while p < n:
        k = p
        while k >= 1:
            for j in range(k % p, n - k, 2 * k):
                for i in range(min(k, n - j - k)):
                    if (i + j) // (2 * p) == (i + j + k) // (2 * p):
                        pairs.append((i + j, i + j + k))
            k //= 2
        p *= 2
    return pairs


def _kth_largest_rows(vals, k):
    sub = 8
    tiles = [vals[r:r + sub, :] for r in range(0, vals.shape[0], sub)]
    for i, j in _sorting_network(len(tiles)):
        hi = jnp.maximum(tiles[i], tiles[j])
        tiles[j] = jnp.minimum(tiles[i], tiles[j])
        tiles[i] = hi
    out = []
    for it in range(k):
        m = jnp.max(tiles[0], axis=0, keepdims=True)
        out.append(m)
        hit = tiles[0] == m
        depth = min(len(tiles), k - it)
        for r in range(depth - 1):
            tiles[r] = jnp.where(hit, tiles[r + 1], tiles[r])
        if depth == len(tiles):
            tiles[depth - 1] = jnp.where(hit, -jnp.inf, tiles[depth - 1])
    return out


def _kth_largest_value(vals, k):
    work = vals
    left = jnp.full((1, vals.shape[1]), float(k), F32)
    kth = jnp.full((1, vals.shape[1]), -jnp.inf, F32)
    for _ in range(k):
        m = jnp.max(work, axis=0, keepdims=True)
        hit = work == m
        kth = jnp.where(left > 0.0, m, kth)
        left = left - jnp.sum(jnp.where(hit, 1.0, 0.0), axis=0, keepdims=True)
        work = jnp.where(hit, -jnp.inf, work)
    return kth


def _staircase_candidates(v1, v2):
    k = PEER_TOPK + 1
    sub = 8
    v1m = jnp.concatenate(v1[:2 * sub], axis=0)
    v2m = jnp.concatenate(v2[:2 * sub], axis=0)
    row = lax.broadcasted_iota(jnp.int32, (sub, v2m.shape[1]), 0)
    blocks = [v1[0] + v2m]
    for i in range(1, sub):
        keep = k // (i + 1)
        blk = v1[i] + v2m[0:sub]
        blocks.append(blk if keep >= sub else jnp.where(row < keep, blk, -jnp.inf))
    blocks.append(v1m[sub:] + v2[0])
    pad = jnp.full((sub - 2, v2m.shape[1]), -jnp.inf, F32)
    blocks.append(jnp.concatenate([v1[2 * sub] + v2[0], v1[0] + v2[2 * sub], pad], axis=0))
    return jnp.concatenate(blocks, axis=0)


def _peer_topk_kernel(qt_ref, k1_ref, k2_ref, s1_ref, s2_ref, st_ref, *, n_heads, n_keys):
    half = k1_ref.shape[1]
    hp = lax.Precision.HIGHEST
    k1 = k1_ref[...]
    k2 = k2_ref[...]
    taus = []
    for h in range(n_heads):
        base = h * 2 * half
        s1 = jnp.dot(k1, qt_ref[base:base + half, :], preferred_element_type=F32, precision=hp)
        s2 = jnp.dot(k2, qt_ref[base + half:base + 2 * half, :], preferred_element_type=F32,
                     precision=hp)
        a1 = s1 * LOG2E
        a2 = s2 * LOG2E
        v1 = _kth_largest_rows(a1, PEER_TOPK + 1)
        v2 = _kth_largest_rows(a2, PEER_TOPK + 1)
        v1s = [v - v1[0] for v in v1]
        v2s = [v - v2[0] for v in v2]
        cand = _staircase_candidates(v1s, v2s)
        picked = cand >= _kth_largest_value(cand, PEER_TOPK)
        zsum = jnp.sum(jnp.where(picked, jnp.exp2(cand), 0.0), axis=0, keepdims=True)
        shift1 = v1[0] + jnp.log2(zsum) + 1.0
        cand_z = _staircase_candidates([v - shift1 for v in v1], v2s)
        last_in = jnp.min(jnp.where(picked, cand_z, jnp.inf), axis=0, keepdims=True)
        first_out = jnp.max(jnp.where(picked, -jnp.inf, cand_z), axis=0, keepdims=True)
        taus.append(0.5 * (last_in + first_out))
        s1_ref[h * n_keys:(h + 1) * n_keys, :] = a1 - shift1
        s2_ref[h * n_keys:(h + 1) * n_keys, :] = a2 - v2[0]
    st_ref[...] = jnp.concatenate(taus, axis=0)


def _peer_topk(qt, k1, k2, tn=256):
    hq, t = qt.shape
    n_keys, half = k1.shape
    n_heads = hq // (2 * half)
    srows = n_heads * n_keys
    col = lambda rows: pl.BlockSpec((rows, tn), lambda i: (0, i))
    return pl.pallas_call(
        functools.partial(_peer_topk_kernel, n_heads=n_heads, n_keys=n_keys),
        grid=(t // tn,),
        in_specs=[col(hq),
                  pl.BlockSpec((n_keys, half), lambda i: (0, 0)),
                  pl.BlockSpec((n_keys, half), lambda i: (0, 0))],
        out_specs=[col(srows), col(srows), col(n_heads)],
        out_shape=[jax.ShapeDtypeStruct((srows, t), F32), jax.ShapeDtypeStruct((srows, t), F32),
                   jax.ShapeDtypeStruct((n_heads, t), F32)],
        compiler_params=_params(("parallel",)),
        name="peer_topk",
    )(qt, k1.astype(F32), k2.astype(F32))


def _peer_dense_kernel(h_ref, u_ref, v_ref, l1_ref, l2_ref, tau_ref, o_ref, tau8, l1b, wt, out_acc,
                       *, n_heads, n_keys):
    j = pl.program_id(1)
    te = v_ref.shape[0]
    tm = h_ref.shape[0]
    d = v_ref.shape[1]
    na = te // n_keys
    sub = 8
    dc = 1024

    @pl.when(j == 0)
    def _():
        out_acc[...] = jnp.zeros_like(out_acc)
        for h in range(n_heads):
            tau8[h * sub:(h + 1) * sub, :] = jnp.broadcast_to(tau_ref[h:h + 1, :], (sub, tm))

    for aa in range(na):
        for h in range(n_heads):
            row = l1_ref[pl.ds(h * n_keys + j * na + aa, 1), :]
            k = (aa * n_heads + h) * sub
            l1b[k:k + sub, :] = jnp.broadcast_to(row, (sub, tm))

    ax = 2
    link = jnp.zeros((sub, tm), F32)
    for a0 in range(0, na, ax):
        for b0 in range(0, n_keys, 2 * sub):
            acc = [[link, link] for _ in range(ax)]
            for h in range(n_heads):
                t8 = tau8[h * sub:(h + 1) * sub, :]
                r2 = [slice(h * n_keys + b0 + y * sub, h * n_keys + b0 + (y + 1) * sub) for y in range(2)]
                l2v = [l2_ref[r, :] for r in r2]
                for x in range(ax):
                    k = ((a0 + x) * n_heads + h) * sub
                    l1v = l1b[k:k + sub, :]
                    for y in range(2):
                        logit = l1v + l2v[y]
                        acc[x][y] = acc[x][y] + jnp.exp2(jnp.where(logit >= t8, logit, -jnp.inf))
            for x in range(ax):
                r0 = (a0 + x) * n_keys + b0
                wt[r0:r0 + 2 * sub, :] = jnp.concatenate(acc[x], axis=0).astype(wt.dtype)
            link = jnp.minimum(pltpu.roll(acc[0][0], 1, 1), 0.0)

    w = wt[...].T
    act = lax.dot_general(h_ref[...], u_ref[...], (((1,), (1,)), ((), ())),
                          preferred_element_type=F32)
    inner = act * (GELU_C0 + GELU_C1 * (act * act))
    g = (act * (1.0 + jnp.tanh(inner))).astype(BF16) * w
    for c0 in range(0, d, dc):
        out_acc[:, c0:c0 + dc] += jnp.dot(g, v_ref[:, c0:c0 + dc], preferred_element_type=F32)

    @pl.when(j == pl.num_programs(1) - 1)
    def _():
        o_ref[...] = out_acc[...].astype(o_ref.dtype)


def _peer_dense(h2, u_tab, v_tab, layer, s1t, s2t, stats, tm=512, te=512):
    t, d = h2.shape
    ne = v_tab.shape[1]
    n_keys = int(round(math.sqrt(ne)))
    n_heads = s1t.shape[0] // n_keys
    srows = s1t.shape[0]
    na = te // n_keys
    tok = lambda rows: pl.BlockSpec((rows, tm), lambda i, j: (0, i))
    return pl.pallas_call(
        functools.partial(_peer_dense_kernel, n_heads=n_heads, n_keys=n_keys),
        grid=(t // tm, ne // te),
        in_specs=[pl.BlockSpec((tm, d), lambda i, j: (i, 0)),
                  pl.BlockSpec((None, te, d), lambda i, j: (layer, j, 0)),
                  pl.BlockSpec((None, te, d), lambda i, j: (layer, j, 0)),
                  tok(srows), tok(srows), tok(n_heads)],
        out_specs=pl.BlockSpec((tm, d), lambda i, j: (i, 0)),
        out_shape=jax.ShapeDtypeStruct((t, d), BF16),
        scratch_shapes=[pltpu.VMEM((n_heads * 8, tm), F32),
                        pltpu.VMEM((na * n_heads * 8, tm), F32),
                        pltpu.VMEM((te, tm), BF16),
                        pltpu.VMEM((tm, d), F32)],
        compiler_params=_params(("parallel", "arbitrary")),
        name="peer_dense",
    )(h2, u_tab, v_tab, s1t, s2t, stats)


def kernel(x, c, w_ada, b_ada, ada_layer, norm1_g, norm2_g, w_in, lam_re, lam_im, log_dt, b_re, b_im,
           c_re, c_im, d_skip, w_glu, q_gain, k_gain, sinks, gn_ssm, gn_attn, w_out, peer_wq, peer_k1,
           peer_k2, peer_u, peer_v):
    bsz, seq, d = x.shape
    t = bsz * seq
    depth = w_in.shape[0]
    n_mod = ada_layer.shape[1]
    ssm_w = w_glu.shape[1]
    attn_w = gn_attn.shape[1]
    kv_w = (w_in.shape[2] - ssm_w - attn_w) // 2

    w_in_b, w_glu_b, w_out_b = w_in, w_glu.astype(BF16), w_out
    wq_t_b = peer_wq.transpose(0, 2, 1).astype(BF16)
    u_b, v_b = peer_u.astype(BF16), peer_v.astype(BF16)

    cond = _cond(c, w_ada, b_ada).reshape(bsz, n_mod, d)
    xf = x.astype(F32).reshape(t, d)
    delta, gate_prev = None, None
    for l in range(depth):
        mod = cond + ada_layer[l].astype(F32)
        shift1, scale1, gate1, shift2, scale2, gate2 = (mod[:, i] for i in range(n_mod))

        xf, h = _adaln_norm(xf, delta, gate_prev, norm1_g[l], scale1, shift1, seq)
        z_ssm = _matmul_channel_blocks(h, w_in_b, l, ssm_w, BF16, tm=1024, tn=512, steps=S5_CHUNK)
        z_attn = _matmul(h, w_in_b, l, ssm_w, attn_w + 2 * kv_w, BF16, tm=1024, tn=512)
        ops = _s5_operators(lam_re[l], lam_im[l], log_dt[l], b_re[l], b_im[l], c_re[l], c_im[l],
                            d_skip[l])
        g = _s5_core(z_ssm, ops, bsz)
        y_ssm = _glu_norm(g, w_glu_b, l, gn_ssm[l], steps=S5_CHUNK)
        y_attn = _attention(z_attn, 0, attn_w, kv_w, q_gain[l], k_gain[l], sinks[l], gn_attn[l], bsz)
        mixed = _matmul_concat(y_ssm, y_attn, w_out_b, l, BF16, tm=1024, tn=512)

        xf, h2 = _adaln_norm(xf, mixed, gate1, norm2_g[l], scale2, shift2, seq)
        qt = _matmul_nt(wq_t_b, l, h2, F32, tn=512)
        s1t, s2t, stats = _peer_topk(qt, peer_k1[l], peer_k2[l])
        delta = _peer_dense(h2, u_b, v_b, l, s1t, s2t, stats)
        gate_prev = gate2
    out = _residual_add(xf, delta, gate_prev, seq)
    return out.reshape(bsz, seq, d).astype(x.dtype)
```

```python
import functools
import math

import jax
import jax.numpy as jnp
from jax import lax
from jax.experimental import pallas as pl
from jax.experimental.pallas import tpu as pltpu

F32 = jnp.float32
BF16 = jnp.bfloat16
EPS = 1e-6
WINDOW = 128
PEER_TOPK = 16
LOG2E = 1.4426950408889634
GELU_C0 = math.sqrt(2.0 / math.pi)
GELU_C1 = 0.044715 * GELU_C0
S5_CHUNK = 16
S5_GROUP_BLOCK = 8
V7X_VMEM_LIMIT = 56 * 1024 * 1024


def _params(semantics, vmem=V7X_VMEM_LIMIT, flags=None):
    return pltpu.CompilerParams(dimension_semantics=semantics, vmem_limit_bytes=vmem, flags=flags)


def _mm_kernel(a_ref, b_ref, o_ref):
    o_ref[...] = jnp.dot(a_ref[...], b_ref[...].astype(a_ref.dtype),
                         preferred_element_type=F32).astype(o_ref.dtype)


def _matmul(a, b, layer, col0, n, out_dtype, tm, tn):
    m, k = a.shape
    j0 = col0 // tn
    return pl.pallas_call(
        _mm_kernel,
        grid=(m // tm, n // tn),
        in_specs=[pl.BlockSpec((tm, k), lambda i, j: (i, 0)),
                  pl.BlockSpec((None, k, tn), lambda i, j: (layer, 0, j0 + j))],
        out_specs=pl.BlockSpec((tm, tn), lambda i, j: (i, j)),
        out_shape=jax.ShapeDtypeStruct((m, n), out_dtype),
        compiler_params=_params(("parallel", "parallel")),
        name="matmul",
    )(a, b)


def _mm_cb_kernel(a_ref, b_ref, o_ref, scr):
    res = jnp.dot(a_ref[...], b_ref[...].astype(a_ref.dtype), preferred_element_type=F32)
    nb, rows, width = o_ref.shape
    lanes = scr.shape[2]
    steps = width // lanes
    for k in range(nb):
        scr[k] = res[:, k * lanes:(k + 1) * lanes]
    for k in range(nb):
        for s in range(steps):
            o_ref[k, :, s * lanes:(s + 1) * lanes] = (
                scr[k, pl.ds(s, rows, stride=steps), :].astype(o_ref.dtype))


def _matmul_channel_blocks(a, b, layer, n, out_dtype, tm, tn, steps, lanes=128):
    m, k = a.shape
    return pl.pallas_call(
        _mm_cb_kernel,
        grid=(m // tm, n // tn),
        in_specs=[pl.BlockSpec((tm, k), lambda i, j: (i, 0)),
                  pl.BlockSpec((None, k, tn), lambda i, j: (layer, 0, j))],
        out_specs=pl.BlockSpec((tn // lanes, tm // steps, steps * lanes), lambda i, j: (j, i, 0)),
        out_shape=jax.ShapeDtypeStruct((n // lanes, m // steps, steps * lanes), out_dtype),
        scratch_shapes=[pltpu.VMEM((tn // lanes, tm, lanes), F32)],
        compiler_params=_params(("parallel", "parallel")),
        name="matmul_channel_blocks",
    )(a, b)


def _mm2_kernel(a1_ref, a2_ref, b1_ref, b2_ref, o_ref):
    acc = jnp.dot(a1_ref[...], b1_ref[...].astype(a1_ref.dtype), preferred_element_type=F32)
    acc += jnp.dot(a2_ref[...], b2_ref[...].astype(a2_ref.dtype), preferred_element_type=F32)
    o_ref[...] = acc.astype(o_ref.dtype)


def _matmul_concat(a1, a2, b, layer, out_dtype, tm, tn):
    m, k1 = a1.shape
    k2 = a2.shape[1]
    assert k1 == k2
    n = b.shape[2]
    return pl.pallas_call(
        _mm2_kernel,
        grid=(m // tm, n // tn),
        in_specs=[pl.BlockSpec((tm, k1), lambda i, j: (i, 0)),
                  pl.BlockSpec((tm, k2), lambda i, j: (i, 0)),
                  pl.BlockSpec((None, k1, tn), lambda i, j: (layer, 0, j)),
                  pl.BlockSpec((None, k2, tn), lambda i, j: (layer, 1, j))],
        out_specs=pl.BlockSpec((tm, tn), lambda i, j: (i, j)),
        out_shape=jax.ShapeDtypeStruct((m, n), out_dtype),
        compiler_params=_params(("parallel", "parallel")),
        name="matmul_concat",
    )(a1, a2, b, b)


def _mm_nt_kernel(a_ref, b_ref, o_ref):
    o_ref[...] = lax.dot_general(a_ref[...], b_ref[...], (((1,), (1,)), ((), ())),
                                 preferred_element_type=F32).astype(o_ref.dtype)


def _matmul_nt(a, layer, b, out_dtype, tn):
    _, m, k = a.shape
    n = b.shape[0]
    return pl.pallas_call(
        _mm_nt_kernel,
        grid=(n // tn,),
        in_specs=[pl.BlockSpec((None, m, k), lambda j: (layer, 0, 0)),
                  pl.BlockSpec((tn, k), lambda j: (j, 0))],
        out_specs=pl.BlockSpec((m, tn), lambda j: (0, j)),
        out_shape=jax.ShapeDtypeStruct((m, n), out_dtype),
        compiler_params=_params(("parallel",)),
        name="matmul_nt",
    )(a, b)


def _cond_kernel(c_ref, w_ref, b_ref, o_ref):
    c = c_ref[...]
    s = c * jax.nn.sigmoid(c)
    o_ref[...] = jnp.dot(s, w_ref[...], preferred_element_type=F32,
                         precision=lax.Precision.HIGHEST) + b_ref[...]


def _cond(c, w_ada, b_ada, tn=512):
    bsz, d = c.shape
    n = w_ada.shape[1]
    rows = 8
    cp = jnp.zeros((rows, d), F32).at[:bsz].set(c.astype(F32))
    out = pl.pallas_call(
        _cond_kernel,
        grid=(n // tn,),
        in_specs=[pl.BlockSpec((rows, d), lambda j: (0, 0)),
                  pl.BlockSpec((d, tn), lambda j: (0, j)),
                  pl.BlockSpec((1, tn), lambda j: (0, j))],
        out_specs=pl.BlockSpec((rows, tn), lambda j: (0, j)),
        out_shape=jax.ShapeDtypeStruct((rows, n), F32),
        compiler_params=_params(("parallel",)),
        name="adaln_cond",
    )(cp, w_ada.astype(F32), b_ada.astype(F32).reshape(1, n))
    return out[:bsz]


def _norm_body(x, g_ref, scale_ref, shift_ref, h_ref):
    y = x * lax.rsqrt(jnp.mean(x * x, axis=-1, keepdims=True) + EPS)
    y = y * g_ref[...]
    h_ref[...] = (y * (1.0 + scale_ref[0]) + shift_ref[0]).astype(h_ref.dtype)


def _norm_kernel(x_ref, g_ref, scale_ref, shift_ref, h_ref):
    _norm_body(x_ref[...], g_ref, scale_ref, shift_ref, h_ref)


def _resnorm_kernel(x_ref, d_ref, gate_ref, g_ref, scale_ref, shift_ref, xo_ref, h_ref):
    x = x_ref[...] + gate_ref[0] * d_ref[...]
    xo_ref[...] = x
    _norm_body(x, g_ref, scale_ref, shift_ref, h_ref)


def _adaln_norm(x, delta, gate, g, scale, shift, rows_per_batch, tm=256):
    t, d = x.shape
    tpb = rows_per_batch // tm
    row = pl.BlockSpec((tm, d), lambda i: (i, 0))
    per_batch = pl.BlockSpec((1, 1, d), lambda i: (i // tpb, 0, 0))
    vec = pl.BlockSpec((1, d), lambda i: (0, 0))
    g2 = g.astype(F32).reshape(1, d)
    b3 = lambda v: v.astype(F32).reshape(v.shape[0], 1, d)
    if delta is None:
        h = pl.pallas_call(
            _norm_kernel,
            grid=(t // tm,),
            in_specs=[row, vec, per_batch, per_batch],
            out_specs=row,
            out_shape=jax.ShapeDtypeStruct((t, d), BF16),
            compiler_params=_params(("parallel",)),
            name="adaln_norm",
        )(x, g2, b3(scale), b3(shift))
        return x, h
    return pl.pallas_call(
        _resnorm_kernel,
        grid=(t // tm,),
        in_specs=[row, row, per_batch, vec, per_batch, per_batch],
        out_specs=[row, row],
        out_shape=[jax.ShapeDtypeStruct((t, d), F32), jax.ShapeDtypeStruct((t, d), BF16)],
        compiler_params=_params(("parallel",)),
        name="residual_adaln_norm",
    )(x, delta, b3(gate), g2, b3(scale), b3(shift))


def _residual_kernel(x_ref, d_ref, gate_ref, o_ref):
    o_ref[...] = x_ref[...] + gate_ref[0] * d_ref[...]


def _residual_add(x, delta, gate, rows_per_batch, tm=256):
    t, d = x.shape
    tpb = rows_per_batch // tm
    row = pl.BlockSpec((tm, d), lambda i: (i, 0))
    return pl.pallas_call(
        _residual_kernel,
        grid=(t // tm,),
        in_specs=[row, row, pl.BlockSpec((1, 1, d), lambda i: (i // tpb, 0, 0))],
        out_specs=row,
        out_shape=jax.ShapeDtypeStruct((t, d), F32),
        compiler_params=_params(("parallel",)),
        name="residual_add",
    )(x, delta, gate.astype(F32).reshape(gate.shape[0], 1, d))


def _s5_operators(lam_re, lam_im, log_dt, b_re, b_im, c_re, c_im, d_skip):
    hp = lax.Precision.HIGHEST
    j = S5_CHUNK
    f = lambda v: v.astype(F32)
    lr, li = f(lam_re), f(lam_im)
    g, p = lr.shape
    h = b_re.shape[-1]
    dt = jnp.exp(f(log_dt))[:, None]
    steps = jnp.arange(j + 1, dtype=F32)[:, None, None]
    pmag = jnp.exp(steps * (lr * dt))
    pr = pmag * jnp.cos(steps * (li * dt))
    pi = pmag * jnp.sin(steps * (li * dt))
    ar, ai = pr[1], pi[1]
    den = lr * lr + li * li
    kr = ((ar - 1.0) * lr + ai * li) / den
    ki = (ai * lr - (ar - 1.0) * li) / den
    br, bi = f(b_re), f(b_im)
    bbr = kr[..., None] * br - ki[..., None] * bi
    bbi = kr[..., None] * bi + ki[..., None] * br
    cr, ci = f(c_re), f(c_im)
    er = pr[:j, :, :, None] * bbr - pi[:j, :, :, None] * bbi
    ei = pr[:j, :, :, None] * bbi + pi[:j, :, :, None] * bbr
    kk = (jnp.einsum('ghp,ngpk->nghk', cr, er, precision=hp)
          - jnp.einsum('ghp,ngpk->nghk', ci, ei, precision=hp))
    kk = kk.at[0].add(f(d_skip)[:, :, None] * jnp.eye(h, dtype=F32))
    gb = S5_GROUP_BLOCK
    nblk = g // gb
    t_op = kk.reshape(j, nblk, gb, h, h).transpose(1, 0, 4, 2, 3).reshape(nblk, j, h, gb * h)

    def chunk_in(e):
        return (e[::-1].reshape(j, nblk, gb, p, h).transpose(1, 0, 4, 2, 3)
                .reshape(nblk, j * h, gb * p))

    p1r, p1i = pr[1:], pi[1:]
    cc_r = cr[None] * p1r[:, :, None, :] - ci[None] * p1i[:, :, None, :]
    cc_i = -(cr[None] * p1i[:, :, None, :] + ci[None] * p1r[:, :, None, :])

    def chunk_out(m):
        return (m.reshape(j, nblk, gb, h, p).transpose(1, 4, 0, 2, 3)
                .reshape(nblk, p, j * gb * h))

    a16r = pr[j].reshape(nblk, 1, gb * p)
    a16i = pi[j].reshape(nblk, 1, gb * p)
    return (t_op.astype(BF16), chunk_in(er).astype(BF16), chunk_in(ei).astype(BF16),
            chunk_out(cc_r).astype(BF16), chunk_out(cc_i).astype(BF16), a16r, a16i)


def _s5_kernel(u_ref, t_ref, bre_ref, bim_ref, cre_ref, cim_ref, ar_ref, ai_ref, mt_ref, mb_ref,
               o_ref, tblk, bblk_r, bblk_i, cblk_r, cblk_i, zre, zim, pre, pim,
               *, chunks_per_batch, n_batch):
    gb = S5_GROUP_BLOCK
    j = S5_CHUNK
    hh = t_ref.shape[2]
    pp = cre_ref.shape[1]
    lanes = gb * hh
    ncol = 4 * lanes
    for i in range(j):
        j_first = (i * lanes // ncol) * ncol // lanes
        for gl in range(gb):
            src = slice(i * hh, (i + 1) * hh)
            dst = slice(i * lanes + gl * hh, i * lanes + (gl + 1) * hh)
            m_gl = mt_ref[gl, :, 0:lanes]
            for jj in range(j_first, j):
                blk = t_ref[0, jj - i] * m_gl if jj >= i else jnp.zeros((hh, lanes), tblk.dtype)
                tblk[dst, jj * lanes:(jj + 1) * lanes] = blk
            bblk_r[dst, :] = bre_ref[0, src, :] * mb_ref[gl]
            bblk_i[dst, :] = bim_ref[0, src, :] * mb_ref[gl]
    for gl in range(gb):
        for r0 in range(0, pp, hh):
            src = slice(r0, r0 + hh)
            dst = slice(gl * pp + r0, gl * pp + r0 + hh)
            cblk_r[dst, :] = cre_ref[0, src, :] * mt_ref[gl]
            cblk_i[dst, :] = cim_ref[0, src, :] * mt_ref[gl]

    u = u_ref[0]
    zre[...] = jnp.dot(u, bblk_r[...], preferred_element_type=F32)
    zim[...] = jnp.dot(u, bblk_i[...], preferred_element_type=F32)
    ar = ar_ref[0]
    ai = ai_ref[0]

    def step(c, carry):
        new = []
        for b in range(n_batch):
            sr, si = carry[2 * b], carry[2 * b + 1]
            row = pl.ds(b * chunks_per_batch + c, 1)
            pre[row, :] = sr
            pim[row, :] = si
            new.append(ar * sr - ai * si + zre[row, :])
            new.append(ar * si + ai * sr + zim[row, :])
        return tuple(new)

    zero = jnp.zeros((1, zre.shape[1]), F32)
    lax.fori_loop(0, chunks_per_batch, step, (zero,) * (2 * n_batch), unroll=8)

    pr_b = pre[...].astype(BF16)
    pi_b = pim[...].astype(BF16)
    for c0 in range(0, j * lanes, ncol):
        cols = slice(c0, c0 + ncol)
        k_rows = c0 + ncol
        y = jnp.dot(u[:, :k_rows], tblk[:k_rows, cols], preferred_element_type=F32)
        y += jnp.dot(pr_b, cblk_r[:, cols], preferred_element_type=F32)
        y += jnp.dot(pi_b, cblk_i[:, cols], preferred_element_type=F32)
        o_ref[0, :, cols] = jax.nn.gelu(y).astype(o_ref.dtype)


def _s5_core(uc, ops, layer, n_batch):
    t_op, bre, bim, cre, cim, a16r, a16i = ops
    j = S5_CHUNK
    gb = S5_GROUP_BLOCK
    nblk, nchunk, width = uc.shape
    lanes = width // j
    hh = lanes // gb
    pp = cre.shape[2]
    def own(width, per_group):
        col_group = jnp.arange(width) // per_group % gb
        m = col_group[None, None, :] == jnp.arange(gb)[:, None, None]
        return jnp.broadcast_to(m, (gb, hh, width)).astype(BF16)

    mask_t = own(j * lanes, hh)
    mask_b = own(gb * pp, pp)
    blk = lambda r, c: pl.BlockSpec((1, r, c), lambda i: (i, 0, 0))
    lblk = lambda r, c: pl.BlockSpec((None, 1, r, c), lambda i: (layer, i, 0, 0))
    const = lambda a: pl.BlockSpec(a.shape, lambda i: (0, 0, 0))
    yc = pl.pallas_call(
        functools.partial(_s5_kernel, chunks_per_batch=nchunk // n_batch, n_batch=n_batch),
        grid=(nblk,),
        in_specs=[blk(nchunk, j * lanes),
                  pl.BlockSpec((None, 1, j, hh, lanes), lambda i: (layer, i, 0, 0, 0)),
                  lblk(j * hh, gb * pp), lblk(j * hh, gb * pp),
                  lblk(pp, j * lanes), lblk(pp, j * lanes),
                  lblk(1, gb * pp), lblk(1, gb * pp), const(mask_t), const(mask_b)],
        out_specs=blk(nchunk, j * lanes),
        out_shape=jax.ShapeDtypeStruct((nblk, nchunk, j * lanes), BF16),
        scratch_shapes=[pltpu.VMEM((j * lanes, j * lanes), BF16),
                        pltpu.VMEM((j * lanes, gb * pp), BF16), pltpu.VMEM((j * lanes, gb * pp), BF16),
                        pltpu.VMEM((gb * pp, j * lanes), BF16), pltpu.VMEM((gb * pp, j * lanes), BF16)]
                       + [pltpu.VMEM((nchunk, gb * pp), F32)] * 4,
        compiler_params=_params(("parallel",)),
        name="s5_chunked_scan",
    )(uc, t_op, bre, bim, cre, cim, a16r, a16i, mask_t, mask_b)
    return yc


def _glu_norm_kernel(g_ref, w_ref, gn_ref, o_ref, scr):
    nblk, rows, width = g_ref.shape
    lanes = scr.shape[2]
    steps = width // lanes
    for cb in range(nblk):
        for s in range(steps):
            scr[cb, pl.ds(s, rows, stride=steps), :] = (
                g_ref[cb, :, s * lanes:(s + 1) * lanes].astype(F32))
    g32 = jnp.concatenate([scr[cb] for cb in range(nblk)], axis=1)
    gate = jax.nn.sigmoid(jnp.dot(g32.astype(BF16), w_ref[...], preferred_element_type=F32))
    y = g32 * gate
    y = y * lax.rsqrt(jnp.mean(y * y, axis=-1, keepdims=True) + EPS)
    o_ref[...] = (y * gn_ref[...]).astype(o_ref.dtype)


def _glu_norm(gc, w_glu, layer, gn, steps, tm=512, lanes=128):
    nblk, nrow, width = gc.shape
    t = nrow * steps
    w = nblk * lanes
    return pl.pallas_call(
        _glu_norm_kernel,
        grid=(t // tm,),
        in_specs=[pl.BlockSpec((nblk, tm // steps, width), lambda i: (0, i, 0)),
                  pl.BlockSpec((None, w, w), lambda i: (layer, 0, 0)),
                  pl.BlockSpec((1, w), lambda i: (0, 0))],
        out_specs=pl.BlockSpec((tm, w), lambda i: (i, 0)),
        out_shape=jax.ShapeDtypeStruct((t, w), BF16),
        scratch_shapes=[pltpu.VMEM((nblk, tm, lanes), F32)],
        compiler_params=_params(("parallel",)),
        name="glu_groupnorm",
    )(gc, w_glu, gn.astype(F32).reshape(1, w))


def _head_rms_scale(x, sel_ref, exp_ref, head_dim):
    ss = jnp.dot((x * x).astype(BF16), sel_ref[...], preferred_element_type=F32)
    r = lax.rsqrt(ss * (1.0 / head_dim) + EPS)
    r_hi = r.astype(BF16)
    r_lo = (r - r_hi.astype(F32)).astype(BF16)
    return (jnp.dot(r_hi, exp_ref[...], preferred_element_type=F32)
            + jnp.dot(r_lo, exp_ref[...], preferred_element_type=F32))


def _attn_kernel(sink_ref, q_ref, kp_ref, kc_ref, vp_ref, vc_ref, qg_ref, kg_ref, gn_ref,
                 qsel_ref, qexp_ref, ksel_ref, kexp_ref, o_ref, acc,
                 *, n_q_heads, n_kv_heads, head_dim):
    nblk = pl.program_id(1)
    w = WINDOW
    lw = 2 * head_dim
    grp = n_q_heads // n_kv_heads
    t_loc = lax.broadcasted_iota(jnp.int32, (w, 2 * w), 0)
    s_loc = lax.broadcasted_iota(jnp.int32, (w, 2 * w), 1)
    dist = t_loc + w - s_loc
    valid = (dist >= 0) & (dist < w) & (s_loc + nblk * w >= w)
    dmask = jnp.where(valid, dist.astype(F32), jnp.inf)

    q = q_ref[...].astype(F32)
    qn = (q * _head_rms_scale(q, qsel_ref, qexp_ref, head_dim) * qg_ref[...]).astype(BF16)
    k = jnp.concatenate([kp_ref[...], kc_ref[...]], axis=0).astype(F32)
    kn = k * _head_rms_scale(k, ksel_ref, kexp_ref, head_dim) * kg_ref[...]
    v = jnp.concatenate([vp_ref[...], vc_ref[...]], axis=0).astype(F32)

    lane = lax.broadcasted_iota(jnp.int32, (2 * w, lw), 1)
    lo = lane < head_dim

    def block_diag(x, hk):
        grp_lanes = x[:, (hk // 2) * lw:(hk // 2 + 1) * lw]
        swapped = pltpu.roll(grp_lanes, head_dim, 1)
        low, high = (grp_lanes, swapped) if hk % 2 == 0 else (swapped, grp_lanes)
        return jnp.concatenate([jnp.where(lo, low, 0.0), jnp.where(lo, 0.0, high)],
                               axis=0).astype(BF16)

    lane_q = lax.broadcasted_iota(jnp.int32, (w, lw), 1) < head_dim
    for hk in range(n_kv_heads):
        kb = block_diag(kn, hk)
        vb = block_diag(v, hk)
        for pq in range(grp // 2):
            pair = hk * (grp // 2) + pq
            cols = slice(pair * lw, (pair + 1) * lw)
            s = lax.dot_general(qn[:, cols], kb, (((1,), (1,)), ((), ())),
                                preferred_element_type=F32)
            es, inv = [], []
            for i in range(2):
                hq = 2 * pair + i
                slope = 2.0 ** (-8.0 * (hq + 1) / n_q_heads)
                si = s[:, i * 2 * w:(i + 1) * 2 * w] - slope * dmask
                sink = sink_ref[hq]
                m = jnp.maximum(jnp.max(si, axis=-1, keepdims=True), sink)
                e = jnp.exp(si - m)
                inv.append(1.0 / (jnp.sum(e, axis=-1, keepdims=True) + jnp.exp(sink - m)))
                es.append(e.astype(BF16))
            pv = jnp.dot(jnp.concatenate(es, axis=1), vb, preferred_element_type=F32)
            acc[:, cols] = pv * jnp.where(lane_q, inv[0], inv[1])
    y = acc[...]
    y = y * lax.rsqrt(jnp.mean(y * y, axis=-1, keepdims=True) + EPS)
    o_ref[...] = (y * gn_ref[...]).astype(o_ref.dtype)


def _attention(z, ssm_w, attn_w, kv_w, q_gain, k_gain, sinks, gn, n_batch):
    t = z.shape[0]
    w = WINDOW
    head_dim = q_gain.shape[0]
    n_q = attn_w // head_dim
    n_kv = kv_w // head_dim
    nb = t // n_batch // w
    qcol = ssm_w // attn_w
    kcol = (ssm_w + attn_w) // kv_w
    vcol = kcol + 1
    cur = lambda col: (lambda b, n: (b * nb + n, col))
    prev = lambda col: (lambda b, n: (b * nb + jnp.maximum(n - 1, 0), col))
    full = lambda r, c: pl.BlockSpec((r, c), lambda b, n: (0, 0))
    nsel = 128

    def selectors(width):
        sel = (jnp.arange(width)[:, None] // head_dim == jnp.arange(nsel)[None, :]).astype(BF16)
        return sel, sel.T

    qsel, qexp = selectors(attn_w)
    ksel, kexp = selectors(kv_w)
    qg_row = (jnp.tile(q_gain.astype(F32), n_q) * head_dim ** -0.5).reshape(1, attn_w)
    kg_row = jnp.tile(k_gain.astype(F32), n_kv).reshape(1, kv_w)
    return pl.pallas_call(
        functools.partial(_attn_kernel, n_q_heads=n_q, n_kv_heads=n_kv, head_dim=head_dim),
        grid=(n_batch, nb),
        in_specs=[pl.BlockSpec(memory_space=pltpu.SMEM),
                  pl.BlockSpec((w, attn_w), cur(qcol)),
                  pl.BlockSpec((w, kv_w), prev(kcol)),
                  pl.BlockSpec((w, kv_w), cur(kcol)),
                  pl.BlockSpec((w, kv_w), prev(vcol)),
                  pl.BlockSpec((w, kv_w), cur(vcol)),
                  full(1, attn_w), full(1, kv_w), full(1, attn_w),
                  full(attn_w, nsel), full(nsel, attn_w), full(kv_w, nsel), full(nsel, kv_w)],
        out_specs=pl.BlockSpec((w, attn_w), lambda b, n: (b * nb + n, 0)),
        out_shape=jax.ShapeDtypeStruct((t, attn_w), BF16),
        scratch_shapes=[pltpu.VMEM((w, attn_w), F32)],
        compiler_params=_params(("parallel", "parallel")),
        name="swa_attention",
    )(sinks.astype(F32), z, z, z, z, z, qg_row, kg_row, gn.astype(F32).reshape(1, attn_w),
      qsel, qexp, ksel, kexp)


def _sorting_network(n):
    pairs = []
    p = 1
    while p < n:
        k = p
        while k >= 1:
            for j in range(k % p, n - k, 2 * k):
                for i in range(min(k, n - j - k)):
                    if (i + j) // (2 * p) == (i + j + k) // (2 * p):
                        pairs.append((i + j, i + j + k))
            k //= 2
        p *= 2
    return pairs


def _kth_largest_rows(vals, k):
    sub = 8
    tiles = [vals[r:r + sub, :] for r in range(0, vals.shape[0], sub)]
    for i, j in _sorting_network(len(tiles)):
        hi = jnp.maximum(tiles[i], tiles[j])
        tiles[j] = jnp.minimum(tiles[i], tiles[j])
        tiles[i] = hi
    out = []
    for it in range(k):
        m = jnp.max(tiles[0], axis=0, keepdims=True)
        out.append(m)
        hit = tiles[0] == m
        depth = min(len(tiles), k - it)
        for r in range(depth - 1):
            tiles[r] = jnp.where(hit, tiles[r + 1], tiles[r])
        if depth == len(tiles):
            tiles[depth - 1] = jnp.where(hit, -jnp.inf, tiles[depth - 1])
    return out


def _kth_largest_value(vals, k):
    work = vals
    left = jnp.full((1, vals.shape[1]), float(k), F32)
    kth = jnp.full((1, vals.shape[1]), -jnp.inf, F32)
    for _ in range(k):
        m = jnp.max(work, axis=0, keepdims=True)
        hit = work == m
        kth = jnp.where(left > 0.0, m, kth)
        left = left - jnp.sum(jnp.where(hit, 1.0, 0.0), axis=0, keepdims=True)
        work = jnp.where(hit, -jnp.inf, work)
    return kth


def _staircase_candidates(v1, v2):
    k = PEER_TOPK + 1
    sub = 8
    v1m = jnp.concatenate(v1[:2 * sub], axis=0)
    v2m = jnp.concatenate(v2[:2 * sub], axis=0)
    row = lax.broadcasted_iota(jnp.int32, (sub, v2m.shape[1]), 0)
    blocks = [v1[0] + v2m]
    for i in range(1, sub):
        keep = k // (i + 1)
        blk = v1[i] + v2m[0:sub]
        blocks.append(blk if keep >= sub else jnp.where(row < keep, blk, -jnp.inf))
    blocks.append(v1m[sub:] + v2[0])
    pad = jnp.full((sub - 2, v2m.shape[1]), -jnp.inf, F32)
    blocks.append(jnp.concatenate([v1[2 * sub] + v2[0], v1[0] + v2[2 * sub], pad], axis=0))
    return jnp.concatenate(blocks, axis=0)


def _peer_topk_kernel(qt_ref, k1_ref, k2_ref, s1_ref, s2_ref, st_ref, *, n_heads, n_keys):
    half = k1_ref.shape[1]
    hp = lax.Precision.HIGHEST
    k1 = k1_ref[...]
    k2 = k2_ref[...]
    taus = []
    for h in range(n_heads):
        base = h * 2 * half
        s1 = jnp.dot(k1, qt_ref[base:base + half, :], preferred_element_type=F32, precision=hp)
        s2 = jnp.dot(k2, qt_ref[base + half:base + 2 * half, :], preferred_element_type=F32,
                     precision=hp)
        a1 = s1 * LOG2E
        a2 = s2 * LOG2E
        v1 = _kth_largest_rows(a1, PEER_TOPK + 1)
        v2 = _kth_largest_rows(a2, PEER_TOPK + 1)
        v1s = [v - v1[0] for v in v1]
        v2s = [v - v2[0] for v in v2]
        cand = _staircase_candidates(v1s, v2s)
        picked = cand >= _kth_largest_value(cand, PEER_TOPK)
        zsum = jnp.sum(jnp.where(picked, jnp.exp2(cand), 0.0), axis=0, keepdims=True)
        shift1 = v1[0] + jnp.log2(zsum) + 1.0
        cand_z = _staircase_candidates([v - shift1 for v in v1], v2s)
        last_in = jnp.min(jnp.where(picked, cand_z, jnp.inf), axis=0, keepdims=True)
        first_out = jnp.max(jnp.where(picked, -jnp.inf, cand_z), axis=0, keepdims=True)
        taus.append(0.5 * (last_in + first_out))
        s1_ref[h * n_keys:(h + 1) * n_keys, :] = a1 - shift1
        s2_ref[h * n_keys:(h + 1) * n_keys, :] = a2 - v2[0]
    st_ref[...] = jnp.concatenate(taus, axis=0)


def _peer_topk(qt, k1, k2, tn=256):
    hq, t = qt.shape
    n_keys, half = k1.shape
    n_heads = hq // (2 * half)
    srows = n_heads * n_keys
    col = lambda rows: pl.BlockSpec((rows, tn), lambda i: (0, i))
    return pl.pallas_call(
        functools.partial(_peer_topk_kernel, n_heads=n_heads, n_keys=n_keys),
        grid=(t // tn,),
        in_specs=[col(hq),
                  pl.BlockSpec((n_keys, half), lambda i: (0, 0)),
                  pl.BlockSpec((n_keys, half), lambda i: (0, 0))],
        out_specs=[col(srows), col(srows), col(n_heads)],
        out_shape=[jax.ShapeDtypeStruct((srows, t), F32), jax.ShapeDtypeStruct((srows, t), F32),
                   jax.ShapeDtypeStruct((n_heads, t), F32)],
        compiler_params=_params(("parallel",)),
        name="peer_topk",
    )(qt, k1.astype(F32), k2.astype(F32))


def _peer_dense_kernel(h_ref, u_ref, v_ref, l1_ref, l2_ref, tau_ref, o_ref, tau8, l1b, wt, out_acc,
                       *, n_heads, n_keys):
    j = pl.program_id(1)
    te = v_ref.shape[0]
    tm = h_ref.shape[0]
    d = v_ref.shape[1]
    na = te // n_keys
    sub = 8
    dc = 1024

    @pl.when(j == 0)
    def _():
        out_acc[...] = jnp.zeros_like(out_acc)
        for h in range(n_heads):
            tau8[h * sub:(h + 1) * sub, :] = jnp.broadcast_to(tau_ref[h:h + 1, :], (sub, tm))

    for aa in range(na):
        for h in range(n_heads):
            row = l1_ref[pl.ds(h * n_keys + j * na + aa, 1), :]
            k = (aa * n_heads + h) * sub
            l1b[k:k + sub, :] = jnp.broadcast_to(row, (sub, tm))

    ax = 2
    link = jnp.zeros((sub, tm), F32)
    for a0 in range(0, na, ax):
        for b0 in range(0, n_keys, 2 * sub):
            acc = [[link, link] for _ in range(ax)]
            for h in range(n_heads):
                t8 = tau8[h * sub:(h + 1) * sub, :]
                r2 = [slice(h * n_keys + b0 + y * sub, h * n_keys + b0 + (y + 1) * sub) for y in range(2)]
                l2v = [l2_ref[r, :] for r in r2]
                for x in range(ax):
                    k = ((a0 + x) * n_heads + h) * sub
                    l1v = l1b[k:k + sub, :]
                    for y in range(2):
                        logit = l1v + l2v[y]
                        acc[x][y] = acc[x][y] + jnp.exp2(jnp.where(logit >= t8, logit, -jnp.inf))
            for x in range(ax):
                r0 = (a0 + x) * n_keys + b0
                wt[r0:r0 + 2 * sub, :] = jnp.concatenate(acc[x], axis=0).astype(wt.dtype)
            link = jnp.minimum(pltpu.roll(acc[0][0], 1, 1), 0.0)

    w = wt[...].T
    act = lax.dot_general(h_ref[...], u_ref[...], (((1,), (1,)), ((), ())),
                          preferred_element_type=F32)
    inner = act * (GELU_C0 + GELU_C1 * (act * act))
    g = (act * (1.0 + jnp.tanh(inner))).astype(BF16) * w
    for c0 in range(0, d, dc):
        out_acc[:, c0:c0 + dc] += jnp.dot(g, v_ref[:, c0:c0 + dc], preferred_element_type=F32)

    @pl.when(j == pl.num_programs(1) - 1)
    def _():
        o_ref[...] = out_acc[...].astype(o_ref.dtype)


def _peer_dense(h2, u_tab, v_tab, layer, s1t, s2t, stats, tm=512, te=512):
    t, d = h2.shape
    ne = v_tab.shape[1]
    n_keys = int(round(math.sqrt(ne)))
    n_heads = s1t.shape[0] // n_keys
    srows = s1t.shape[0]
    na = te // n_keys
    tok = lambda rows: pl.BlockSpec((rows, tm), lambda i, j: (0, i))
    return pl.pallas_call(
        functools.partial(_peer_dense_kernel, n_heads=n_heads, n_keys=n_keys),
        grid=(t // tm, ne // te),
        in_specs=[pl.BlockSpec((tm, d), lambda i, j: (i, 0)),
                  pl.BlockSpec((None, te, d), lambda i, j: (layer, j, 0)),
                  pl.BlockSpec((None, te, d), lambda i, j: (layer, j, 0)),
                  tok(srows), tok(srows), tok(n_heads)],
        out_specs=pl.BlockSpec((tm, d), lambda i, j: (i, 0)),
        out_shape=jax.ShapeDtypeStruct((t, d), BF16),
        scratch_shapes=[pltpu.VMEM((n_heads * 8, tm), F32),
                        pltpu.VMEM((na * n_heads * 8, tm), F32),
                        pltpu.VMEM((te, tm), BF16),
                        pltpu.VMEM((tm, d), F32)],
        compiler_params=_params(("parallel", "arbitrary")),
        name="peer_dense",
    )(h2, u_tab, v_tab, s1t, s2t, stats)


def kernel(x, c, w_ada, b_ada, ada_layer, norm1_g, norm2_g, w_in, lam_re, lam_im, log_dt, b_re, b_im,
           c_re, c_im, d_skip, w_glu, q_gain, k_gain, sinks, gn_ssm, gn_attn, w_out, peer_wq, peer_k1,
           peer_k2, peer_u, peer_v):
    bsz, seq, d = x.shape
    t = bsz * seq
    depth = w_in.shape[0]
    n_mod = ada_layer.shape[1]
    ssm_w = w_glu.shape[1]
    attn_w = gn_attn.shape[1]
    kv_w = (w_in.shape[2] - ssm_w - attn_w) // 2

    w_in_b, w_glu_b, w_out_b = w_in, w_glu.astype(BF16), w_out
    wq_t_b = peer_wq.transpose(0, 2, 1).astype(BF16)
    u_b, v_b = peer_u.astype(BF16), peer_v.astype(BF16)

    s5_ops = jax.vmap(_s5_operators)(lam_re, lam_im, log_dt, b_re, b_im, c_re, c_im, d_skip)

    cond = _cond(c, w_ada, b_ada).reshape(bsz, n_mod, d)
    xf = x.astype(F32).reshape(t, d)
    delta, gate_prev = None, None
    for l in range(depth):
        mod = cond + ada_layer[l].astype(F32)
        shift1, scale1, gate1, shift2, scale2, gate2 = (mod[:, i] for i in range(n_mod))

        xf, h = _adaln_norm(xf, delta, gate_prev, norm1_g[l], scale1, shift1, seq)
        z_ssm = _matmul_channel_blocks(h, w_in_b, l, ssm_w, BF16, tm=1024, tn=512, steps=S5_CHUNK)
        z_attn = _matmul(h, w_in_b, l, ssm_w, attn_w + 2 * kv_w, BF16, tm=1024, tn=512)
        g = _s5_core(z_ssm, s5_ops, l, bsz)
        y_ssm = _glu_norm(g, w_glu_b, l, gn_ssm[l], steps=S5_CHUNK)
        y_attn = _attention(z_attn, 0, attn_w, kv_w, q_gain[l], k_gain[l], sinks[l], gn_attn[l], bsz)
        mixed = _matmul_concat(y_ssm, y_attn, w_out_b, l, BF16, tm=1024, tn=512)

        xf, h2 = _adaln_norm(xf, mixed, gate1, norm2_g[l], scale2, shift2, seq)
        qt = _matmul_nt(wq_t_b, l, h2, F32, tn=512)
        s1t, s2t, stats = _peer_topk(qt, peer_k1[l], peer_k2[l])
        delta = _peer_dense(h2, u_b, v_b, l, s1t, s2t, stats)
        gate_prev = gate2
    out = _residual_add(xf, delta, gate_prev, seq)
    return out.reshape(bsz, seq, d).astype(x.dtype)
```

```python
import functools
import math

import jax
import jax.numpy as jnp
from jax import lax
from jax.experimental import pallas as pl
from jax.experimental.pallas import tpu as pltpu

F32 = jnp.float32
BF16 = jnp.bfloat16
EPS = 1e-6
WINDOW = 128
PEER_TOPK = 16
LOG2E = 1.4426950408889634
GELU_C0 = math.sqrt(2.0 / math.pi)
GELU_C1 = 0.044715 * GELU_C0
S5_CHUNK = 16
S5_GROUP_BLOCK = 8
V7X_VMEM_LIMIT = 56 * 1024 * 1024


def _params(semantics, vmem=V7X_VMEM_LIMIT, flags=None):
    return pltpu.CompilerParams(dimension_semantics=semantics, vmem_limit_bytes=vmem, flags=flags)


def _mm_kernel(a_ref, b_ref, o_ref):
    o_ref[...] = jnp.dot(a_ref[...], b_ref[...].astype(a_ref.dtype),
                         preferred_element_type=F32).astype(o_ref.dtype)


def _matmul(a, b, layer, col0, n, out_dtype, tm, tn):
    m, k = a.shape
    j0 = col0 // tn
    return pl.pallas_call(
        _mm_kernel,
        grid=(m // tm, n // tn),
        in_specs=[pl.BlockSpec((tm, k), lambda i, j: (i, 0)),
                  pl.BlockSpec((None, k, tn), lambda i, j: (layer, 0, j0 + j))],
        out_specs=pl.BlockSpec((tm, tn), lambda i, j: (i, j)),
        out_shape=jax.ShapeDtypeStruct((m, n), out_dtype),
        compiler_params=_params(("parallel", "parallel")),
        name="matmul",
    )(a, b)


def _mm_cb_kernel(a_ref, b_ref, o_ref, scr):
    res = jnp.dot(a_ref[...], b_ref[...].astype(a_ref.dtype), preferred_element_type=F32)
    nb, rows, width = o_ref.shape
    lanes = scr.shape[2]
    steps = width // lanes
    for k in range(nb):
        scr[k] = res[:, k * lanes:(k + 1) * lanes]
    for k in range(nb):
        for s in range(steps):
            o_ref[k, :, s * lanes:(s + 1) * lanes] = (
                scr[k, pl.ds(s, rows, stride=steps), :].astype(o_ref.dtype))


def _matmul_channel_blocks(a, b, layer, n, out_dtype, tm, tn, steps, lanes=128):
    m, k = a.shape
    return pl.pallas_call(
        _mm_cb_kernel,
        grid=(m // tm, n // tn),
        in_specs=[pl.BlockSpec((tm, k), lambda i, j: (i, 0)),
                  pl.BlockSpec((None, k, tn), lambda i, j: (layer, 0, j))],
        out_specs=pl.BlockSpec((tn // lanes, tm // steps, steps * lanes), lambda i, j: (j, i, 0)),
        out_shape=jax.ShapeDtypeStruct((n // lanes, m // steps, steps * lanes), out_dtype),
        scratch_shapes=[pltpu.VMEM((tn // lanes, tm, lanes), F32)],
        compiler_params=_params(("parallel", "parallel")),
        name="matmul_channel_blocks",
    )(a, b)


def _mm2_kernel(a1_ref, a2_ref, b1_ref, b2_ref, o_ref):
    acc = jnp.dot(a1_ref[...], b1_ref[...].astype(a1_ref.dtype), preferred_element_type=F32)
    acc += jnp.dot(a2_ref[...], b2_ref[...].astype(a2_ref.dtype), preferred_element_type=F32)
    o_ref[...] = acc.astype(o_ref.dtype)


def _matmul_concat(a1, a2, b, layer, out_dtype, tm, tn):
    m, k1 = a1.shape
    k2 = a2.shape[1]
    assert k1 == k2
    n = b.shape[2]
    return pl.pallas_call(
        _mm2_kernel,
        grid=(m // tm, n // tn),
        in_specs=[pl.BlockSpec((tm, k1), lambda i, j: (i, 0)),
                  pl.BlockSpec((tm, k2), lambda i, j: (i, 0)),
                  pl.BlockSpec((None, k1, tn), lambda i, j: (layer, 0, j)),
                  pl.BlockSpec((None, k2, tn), lambda i, j: (layer, 1, j))],
        out_specs=pl.BlockSpec((tm, tn), lambda i, j: (i, j)),
        out_shape=jax.ShapeDtypeStruct((m, n), out_dtype),
        compiler_params=_params(("parallel", "parallel")),
        name="matmul_concat",
    )(a1, a2, b, b)


def _mm_nt_kernel(a_ref, b_ref, o_ref):
    o_ref[...] = lax.dot_general(a_ref[...], b_ref[...], (((1,), (1,)), ((), ())),
                                 preferred_element_type=F32).astype(o_ref.dtype)


def _matmul_nt(a, layer, b, out_dtype, tn):
    _, m, k = a.shape
    n = b.shape[0]
    return pl.pallas_call(
        _mm_nt_kernel,
        grid=(n // tn,),
        in_specs=[pl.BlockSpec((None, m, k), lambda j: (layer, 0, 0)),
                  pl.BlockSpec((tn, k), lambda j: (j, 0))],
        out_specs=pl.BlockSpec((m, tn), lambda j: (0, j)),
        out_shape=jax.ShapeDtypeStruct((m, n), out_dtype),
        compiler_params=_params(("parallel",)),
        name="matmul_nt",
    )(a, b)


def _cond_kernel(c_ref, w_ref, b_ref, o_ref):
    c = c_ref[...]
    s = c * jax.nn.sigmoid(c)
    o_ref[...] = jnp.dot(s, w_ref[...], preferred_element_type=F32,
                         precision=lax.Precision.HIGHEST) + b_ref[...]


def _cond(c, w_ada, b_ada, tn=512):
    bsz, d = c.shape
    n = w_ada.shape[1]
    rows = 8
    cp = jnp.zeros((rows, d), F32).at[:bsz].set(c.astype(F32))
    out = pl.pallas_call(
        _cond_kernel,
        grid=(n // tn,),
        in_specs=[pl.BlockSpec((rows, d), lambda j: (0, 0)),
                  pl.BlockSpec((d, tn), lambda j: (0, j)),
                  pl.BlockSpec((1, tn), lambda j: (0, j))],
        out_specs=pl.BlockSpec((rows, tn), lambda j: (0, j)),
        out_shape=jax.ShapeDtypeStruct((rows, n), F32),
        compiler_params=_params(("parallel",)),
        name="adaln_cond",
    )(cp, w_ada.astype(F32), b_ada.astype(F32).reshape(1, n))
    return out[:bsz]


def _norm_body(x, g_ref, scale_ref, shift_ref, h_ref):
    y = x * lax.rsqrt(jnp.mean(x * x, axis=-1, keepdims=True) + EPS)
    y = y * g_ref[...]
    h_ref[...] = (y * (1.0 + scale_ref[0]) + shift_ref[0]).astype(h_ref.dtype)


def _norm_kernel(x_ref, g_ref, scale_ref, shift_ref, h_ref):
    _norm_body(x_ref[...], g_ref, scale_ref, shift_ref, h_ref)


def _resnorm_kernel(x_ref, d_ref, gate_ref, g_ref, scale_ref, shift_ref, xo_ref, h_ref):
    x = x_ref[...] + gate_ref[0] * d_ref[...]
    xo_ref[...] = x
    _norm_body(x, g_ref, scale_ref, shift_ref, h_ref)


def _adaln_norm(x, delta, gate, g, scale, shift, rows_per_batch, tm=256):
    t, d = x.shape
    tpb = rows_per_batch // tm
    row = pl.BlockSpec((tm, d), lambda i: (i, 0))
    per_batch = pl.BlockSpec((1, 1, d), lambda i: (i // tpb, 0, 0))
    vec = pl.BlockSpec((1, d), lambda i: (0, 0))
    g2 = g.astype(F32).reshape(1, d)
    b3 = lambda v: v.astype(F32).reshape(v.shape[0], 1, d)
    if delta is None:
        h = pl.pallas_call(
            _norm_kernel,
            grid=(t // tm,),
            in_specs=[row, vec, per_batch, per_batch],
            out_specs=row,
            out_shape=jax.ShapeDtypeStruct((t, d), BF16),
            compiler_params=_params(("parallel",)),
            name="adaln_norm",
        )(x, g2, b3(scale), b3(shift))
        return x, h
    return pl.pallas_call(
        _resnorm_kernel,
        grid=(t // tm,),
        in_specs=[row, row, per_batch, vec, per_batch, per_batch],
        out_specs=[row, row],
        out_shape=[jax.ShapeDtypeStruct((t, d), F32), jax.ShapeDtypeStruct((t, d), BF16)],
        compiler_params=_params(("parallel",)),
        name="residual_adaln_norm",
    )(x, delta, b3(gate), g2, b3(scale), b3(shift))


def _residual_kernel(x_ref, d_ref, gate_ref, o_ref):
    o_ref[...] = x_ref[...] + gate_ref[0] * d_ref[...]


def _residual_add(x, delta, gate, rows_per_batch, tm=256):
    t, d = x.shape
    tpb = rows_per_batch // tm
    row = pl.BlockSpec((tm, d), lambda i: (i, 0))
    return pl.pallas_call(
        _residual_kernel,
        grid=(t // tm,),
        in_specs=[row, row, pl.BlockSpec((1, 1, d), lambda i: (i // tpb, 0, 0))],
        out_specs=row,
        out_shape=jax.ShapeDtypeStruct((t, d), F32),
        compiler_params=_params(("parallel",)),
        name="residual_add",
    )(x, delta, gate.astype(F32).reshape(gate.shape[0], 1, d))


def _s5_operators(lam_re, lam_im, log_dt, b_re, b_im, c_re, c_im, d_skip):
    hp = lax.Precision.HIGHEST
    j = S5_CHUNK
    f = lambda v: v.astype(F32)
    lr, li = f(lam_re), f(lam_im)
    g, p = lr.shape
    h = b_re.shape[-1]
    dt = jnp.exp(f(log_dt))[:, None]
    steps = jnp.arange(j + 1, dtype=F32)[:, None, None]
    pmag = jnp.exp(steps * (lr * dt))
    pr = pmag * jnp.cos(steps * (li * dt))
    pi = pmag * jnp.sin(steps * (li * dt))
    ar, ai = pr[1], pi[1]
    den = lr * lr + li * li
    kr = ((ar - 1.0) * lr + ai * li) / den
    ki = (ai * lr - (ar - 1.0) * li) / den
    br, bi = f(b_re), f(b_im)
    bbr = kr[..., None] * br - ki[..., None] * bi
    bbi = kr[..., None] * bi + ki[..., None] * br
    cr, ci = f(c_re), f(c_im)
    car = cr[None] * pr[:j, :, None, :] - ci[None] * pi[:j, :, None, :]
    cai = cr[None] * pi[:j, :, None, :] + ci[None] * pr[:j, :, None, :]
    kk = (jnp.einsum('nghp,gpk->nghk', car, bbr, precision=hp)
          - jnp.einsum('nghp,gpk->nghk', cai, bbi, precision=hp))
    kk = kk.at[0].add(f(d_skip)[:, :, None] * jnp.eye(h, dtype=F32))
    gb = S5_GROUP_BLOCK
    nblk = g // gb
    t_op = kk.reshape(j, nblk, gb, h, h).transpose(1, 0, 4, 2, 3).reshape(nblk, j, h, gb * h)

    prr = pr[j - 1::-1].reshape(j, nblk, gb, p).transpose(1, 0, 2, 3)[:, :, None]
    pir = pi[j - 1::-1].reshape(j, nblk, gb, p).transpose(1, 0, 2, 3)[:, :, None]
    bbr_e = bbr.reshape(nblk, gb, p, h).transpose(0, 3, 1, 2)[:, None]
    bbi_e = bbi.reshape(nblk, gb, p, h).transpose(0, 3, 1, 2)[:, None]
    bc_r = (prr * bbr_e - pir * bbi_e).reshape(nblk, j * h, gb * p)
    bc_i = (prr * bbi_e + pir * bbr_e).reshape(nblk, j * h, gb * p)

    cr_e = cr.reshape(nblk, gb, h, p).transpose(0, 3, 1, 2)[:, :, None]
    ci_e = ci.reshape(nblk, gb, h, p).transpose(0, 3, 1, 2)[:, :, None]
    p1r = pr[1:].reshape(j, nblk, gb, p).transpose(1, 3, 0, 2)[..., None]
    p1i = pi[1:].reshape(j, nblk, gb, p).transpose(1, 3, 0, 2)[..., None]
    cc_r = (cr_e * p1r - ci_e * p1i).reshape(nblk, p, j * gb * h)
    cc_i = (-(cr_e * p1i + ci_e * p1r)).reshape(nblk, p, j * gb * h)

    a16r = pr[j].reshape(nblk, 1, gb * p)
    a16i = pi[j].reshape(nblk, 1, gb * p)
    return (t_op.astype(BF16), bc_r.astype(BF16), bc_i.astype(BF16),
            cc_r.astype(BF16), cc_i.astype(BF16), a16r, a16i)


def _s5_kernel(u_ref, t_ref, bre_ref, bim_ref, cre_ref, cim_ref, ar_ref, ai_ref, mt_ref, mb_ref,
               o_ref, tblk, bblk_r, bblk_i, cblk_r, cblk_i, zre, zim, pre, pim,
               *, chunks_per_batch, n_batch):
    gb = S5_GROUP_BLOCK
    j = S5_CHUNK
    hh = t_ref.shape[2]
    pp = cre_ref.shape[1]
    lanes = gb * hh
    ncol = 4 * lanes
    for i in range(j):
        j_first = (i * lanes // ncol) * ncol // lanes
        for gl in range(gb):
            src = slice(i * hh, (i + 1) * hh)
            dst = slice(i * lanes + gl * hh, i * lanes + (gl + 1) * hh)
            m_gl = mt_ref[gl, :, 0:lanes]
            for jj in range(j_first, j):
                blk = t_ref[0, jj - i] * m_gl if jj >= i else jnp.zeros((hh, lanes), tblk.dtype)
                tblk[dst, jj * lanes:(jj + 1) * lanes] = blk
            bblk_r[dst, :] = bre_ref[0, src, :] * mb_ref[gl]
            bblk_i[dst, :] = bim_ref[0, src, :] * mb_ref[gl]
    for gl in range(gb):
        for r0 in range(0, pp, hh):
            src = slice(r0, r0 + hh)
            dst = slice(gl * pp + r0, gl * pp + r0 + hh)
            cblk_r[dst, :] = cre_ref[0, src, :] * mt_ref[gl]
            cblk_i[dst, :] = cim_ref[0, src, :] * mt_ref[gl]

    u = u_ref[0]
    zre[...] = jnp.dot(u, bblk_r[...], preferred_element_type=F32)
    zim[...] = jnp.dot(u, bblk_i[...], preferred_element_type=F32)
    ar = ar_ref[0]
    ai = ai_ref[0]

    def step(c, carry):
        new = []
        for b in range(n_batch):
            sr, si = carry[2 * b], carry[2 * b + 1]
            row = pl.ds(b * chunks_per_batch + c, 1)
            pre[row, :] = sr
            pim[row, :] = si
            new.append(ar * sr - ai * si + zre[row, :])
            new.append(ar * si + ai * sr + zim[row, :])
        return tuple(new)

    zero = jnp.zeros((1, zre.shape[1]), F32)
    lax.fori_loop(0, chunks_per_batch, step, (zero,) * (2 * n_batch), unroll=8)

    pr_b = pre[...].astype(BF16)
    pi_b = pim[...].astype(BF16)
    for c0 in range(0, j * lanes, ncol):
        cols = slice(c0, c0 + ncol)
        k_rows = c0 + ncol
        y = jnp.dot(u[:, :k_rows], tblk[:k_rows, cols], preferred_element_type=F32)
        y += jnp.dot(pr_b, cblk_r[:, cols], preferred_element_type=F32)
        y += jnp.dot(pi_b, cblk_i[:, cols], preferred_element_type=F32)
        o_ref[0, :, cols] = jax.nn.gelu(y).astype(o_ref.dtype)


def _s5_core(uc, ops, layer, n_batch):
    t_op, bre, bim, cre, cim, a16r, a16i = ops
    j = S5_CHUNK
    gb = S5_GROUP_BLOCK
    nblk, nchunk, width = uc.shape
    lanes = width // j
    hh = lanes // gb
    pp = cre.shape[2]
    def own(width, per_group):
        col_group = jnp.arange(width) // per_group % gb
        m = col_group[None, None, :] == jnp.arange(gb)[:, None, None]
        return jnp.broadcast_to(m, (gb, hh, width)).astype(BF16)

    mask_t = own(j * lanes, hh)
    mask_b = own(gb * pp, pp)
    blk = lambda r, c: pl.BlockSpec((1, r, c), lambda i: (i, 0, 0))
    lblk = lambda r, c: pl.BlockSpec((None, 1, r, c), lambda i: (layer, i, 0, 0))
    const = lambda a: pl.BlockSpec(a.shape, lambda i: (0, 0, 0))
    yc = pl.pallas_call(
        functools.partial(_s5_kernel, chunks_per_batch=nchunk // n_batch, n_batch=n_batch),
        grid=(nblk,),
        in_specs=[blk(nchunk, j * lanes),
                  pl.BlockSpec((None, 1, j, hh, lanes), lambda i: (layer, i, 0, 0, 0)),
                  lblk(j * hh, gb * pp), lblk(j * hh, gb * pp),
                  lblk(pp, j * lanes), lblk(pp, j * lanes),
                  lblk(1, gb * pp), lblk(1, gb * pp), const(mask_t), const(mask_b)],
        out_specs=blk(nchunk, j * lanes),
        out_shape=jax.ShapeDtypeStruct((nblk, nchunk, j * lanes), BF16),
        scratch_shapes=[pltpu.VMEM((j * lanes, j * lanes), BF16),
                        pltpu.VMEM((j * lanes, gb * pp), BF16), pltpu.VMEM((j * lanes, gb * pp), BF16),
                        pltpu.VMEM((gb * pp, j * lanes), BF16), pltpu.VMEM((gb * pp, j * lanes), BF16)]
                       + [pltpu.VMEM((nchunk, gb * pp), F32)] * 4,
        compiler_params=_params(("parallel",)),
        name="s5_chunked_scan",
    )(uc, t_op, bre, bim, cre, cim, a16r, a16i, mask_t, mask_b)
    return yc


def _glu_norm_kernel(g_ref, w_ref, gn_ref, o_ref, scr):
    nblk, rows, width = g_ref.shape
    lanes = scr.shape[2]
    steps = width // lanes
    for cb in range(nblk):
        for s in range(steps):
            scr[cb, pl.ds(s, rows, stride=steps), :] = (
                g_ref[cb, :, s * lanes:(s + 1) * lanes].astype(F32))
    g32 = jnp.concatenate([scr[cb] for cb in range(nblk)], axis=1)
    gate = jax.nn.sigmoid(jnp.dot(g32.astype(BF16), w_ref[...], preferred_element_type=F32))
    y = g32 * gate
    y = y * lax.rsqrt(jnp.mean(y * y, axis=-1, keepdims=True) + EPS)
    o_ref[...] = (y * gn_ref[...]).astype(o_ref.dtype)


def _glu_norm(gc, w_glu, layer, gn, steps, tm=512, lanes=128):
    nblk, nrow, width = gc.shape
    t = nrow * steps
    w = nblk * lanes
    return pl.pallas_call(
        _glu_norm_kernel,
        grid=(t // tm,),
        in_specs=[pl.BlockSpec((nblk, tm // steps, width), lambda i: (0, i, 0)),
                  pl.BlockSpec((None, w, w), lambda i: (layer, 0, 0)),
                  pl.BlockSpec((1, w), lambda i: (0, 0))],
        out_specs=pl.BlockSpec((tm, w), lambda i: (i, 0)),
        out_shape=jax.ShapeDtypeStruct((t, w), BF16),
        scratch_shapes=[pltpu.VMEM((nblk, tm, lanes), F32)],
        compiler_params=_params(("parallel",)),
        name="glu_groupnorm",
    )(gc, w_glu, gn.astype(F32).reshape(1, w))


def _head_rms_scale(x, sel_ref, exp_ref, head_dim):
    ss = jnp.dot((x * x).astype(BF16), sel_ref[...], preferred_element_type=F32)
    r = lax.rsqrt(ss * (1.0 / head_dim) + EPS)
    r_hi = r.astype(BF16)
    r_lo = (r - r_hi.astype(F32)).astype(BF16)
    return (jnp.dot(r_hi, exp_ref[...], preferred_element_type=F32)
            + jnp.dot(r_lo, exp_ref[...], preferred_element_type=F32))


def _attn_kernel(sink_ref, q_ref, kp_ref, kc_ref, vp_ref, vc_ref, qg_ref, kg_ref, gn_ref,
                 qsel_ref, qexp_ref, ksel_ref, kexp_ref, o_ref, acc,
                 *, n_q_heads, n_kv_heads, head_dim):
    nblk = pl.program_id(1)
    w = WINDOW
    lw = 2 * head_dim
    grp = n_q_heads // n_kv_heads
    t_loc = lax.broadcasted_iota(jnp.int32, (w, 2 * w), 0)
    s_loc = lax.broadcasted_iota(jnp.int32, (w, 2 * w), 1)
    dist = t_loc + w - s_loc
    valid = (dist >= 0) & (dist < w) & (s_loc + nblk * w >= w)
    dmask = jnp.where(valid, dist.astype(F32), jnp.inf)

    q = q_ref[...].astype(F32)
    qn = (q * _head_rms_scale(q, qsel_ref, qexp_ref, head_dim) * qg_ref[...]).astype(BF16)
    k = jnp.concatenate([kp_ref[...], kc_ref[...]], axis=0).astype(F32)
    kn = k * _head_rms_scale(k, ksel_ref, kexp_ref, head_dim) * kg_ref[...]
    v = jnp.concatenate([vp_ref[...], vc_ref[...]], axis=0).astype(F32)

    lane = lax.broadcasted_iota(jnp.int32, (2 * w, lw), 1)
    lo = lane < head_dim

    def block_diag(x, hk):
        grp_lanes = x[:, (hk // 2) * lw:(hk // 2 + 1) * lw]
        swapped = pltpu.roll(grp_lanes, head_dim, 1)
        low, high = (grp_lanes, swapped) if hk % 2 == 0 else (swapped, grp_lanes)
        return jnp.concatenate([jnp.where(lo, low, 0.0), jnp.where(lo, 0.0, high)],
                               axis=0).astype(BF16)

    lane_q = lax.broadcasted_iota(jnp.int32, (w, lw), 1) < head_dim
    for hk in range(n_kv_heads):
        kb = block_diag(kn, hk)
        vb = block_diag(v, hk)
        for pq in range(grp // 2):
            pair = hk * (grp // 2) + pq
            cols = slice(pair * lw, (pair + 1) * lw)
            s = lax.dot_general(qn[:, cols], kb, (((1,), (1,)), ((), ())),
                                preferred_element_type=F32)
            es, inv = [], []
            for i in range(2):
                hq = 2 * pair + i
                slope = 2.0 ** (-8.0 * (hq + 1) / n_q_heads)
                si = s[:, i * 2 * w:(i + 1) * 2 * w] - slope * dmask
                sink = sink_ref[hq]
                m = jnp.maximum(jnp.max(si, axis=-1, keepdims=True), sink)
                e = jnp.exp(si - m)
                inv.append(1.0 / (jnp.sum(e, axis=-1, keepdims=True) + jnp.exp(sink - m)))
                es.append(e.astype(BF16))
            pv = jnp.dot(jnp.concatenate(es, axis=1), vb, preferred_element_type=F32)
            acc[:, cols] = pv * jnp.where(lane_q, inv[0], inv[1])
    y = acc[...]
    y = y * lax.rsqrt(jnp.mean(y * y, axis=-1, keepdims=True) + EPS)
    o_ref[...] = (y * gn_ref[...]).astype(o_ref.dtype)


def _attention(z, ssm_w, attn_w, kv_w, q_gain, k_gain, sinks, gn, n_batch):
    t = z.shape[0]
    w = WINDOW
    head_dim = q_gain.shape[0]
    n_q = attn_w // head_dim
    n_kv = kv_w // head_dim
    nb = t // n_batch // w
    qcol = ssm_w // attn_w
    kcol = (ssm_w + attn_w) // kv_w
    vcol = kcol + 1
    cur = lambda col: (lambda b, n: (b * nb + n, col))
    prev = lambda col: (lambda b, n: (b * nb + jnp.maximum(n - 1, 0), col))
    full = lambda r, c: pl.BlockSpec((r, c), lambda b, n: (0, 0))
    nsel = 128

    def selectors(width):
        sel = (jnp.arange(width)[:, None] // head_dim == jnp.arange(nsel)[None, :]).astype(BF16)
        return sel, sel.T

    qsel, qexp = selectors(attn_w)
    ksel, kexp = selectors(kv_w)
    qg_row = (jnp.tile(q_gain.astype(F32), n_q) * head_dim ** -0.5).reshape(1, attn_w)
    kg_row = jnp.tile(k_gain.astype(F32), n_kv).reshape(1, kv_w)
    return pl.pallas_call(
        functools.partial(_attn_kernel, n_q_heads=n_q, n_kv_heads=n_kv, head_dim=head_dim),
        grid=(n_batch, nb),
        in_specs=[pl.BlockSpec(memory_space=pltpu.SMEM),
                  pl.BlockSpec((w, attn_w), cur(qcol)),
                  pl.BlockSpec((w, kv_w), prev(kcol)),
                  pl.BlockSpec((w, kv_w), cur(kcol)),
                  pl.BlockSpec((w, kv_w), prev(vcol)),
                  pl.BlockSpec((w, kv_w), cur(vcol)),
                  full(1, attn_w), full(1, kv_w), full(1, attn_w),
                  full(attn_w, nsel), full(nsel, attn_w), full(kv_w, nsel), full(nsel, kv_w)],
        out_specs=pl.BlockSpec((w, attn_w), lambda b, n: (b * nb + n, 0)),
        out_shape=jax.ShapeDtypeStruct((t, attn_w), BF16),
        scratch_shapes=[pltpu.VMEM((w, attn_w), F32)],
        compiler_params=_params(("parallel", "parallel")),
        name="swa_attention",
    )(sinks.astype(F32), z, z, z, z, z, qg_row, kg_row, gn.astype(F32).reshape(1, attn_w),
      qsel, qexp, ksel, kexp)


def _sorting_network(n):
    pairs = []
    p = 1
    while p < n:
        k = p
        while k >= 1:
            for j in range(k % p, n - k, 2 * k):
                for i in range(min(k, n - j - k)):
                    if (i + j) // (2 * p) == (i + j + k) // (2 * p):
                        pairs.append((i + j, i + j + k))
            k //= 2
        p *= 2
    return pairs


def _kth_largest_rows(vals, k):
    sub = 8
    tiles = [vals[r:r + sub, :] for r in range(0, vals.shape[0], sub)]
    for i, j in _sorting_network(len(tiles)):
        hi = jnp.maximum(tiles[i], tiles[j])
        tiles[j] = jnp.minimum(tiles[i], tiles[j])
        tiles[i] = hi
    out = []
    for it in range(k):
        m = jnp.max(tiles[0], axis=0, keepdims=True)
        out.append(m)
        hit = tiles[0] == m
        depth = min(len(tiles), k - it)
        for r in range(depth - 1):
            tiles[r] = jnp.where(hit, tiles[r + 1], tiles[r])
        if depth == len(tiles):
            tiles[depth - 1] = jnp.where(hit, -jnp.inf, tiles[depth - 1])
    return out


def _kth_largest_value(vals, k):
    work = vals
    left = jnp.full((1, vals.shape[1]), float(k), F32)
    kth = jnp.full((1, vals.shape[1]), -jnp.inf, F32)
    for _ in range(k):
        m = jnp.max(work, axis=0, keepdims=True)
        hit = work == m
        kth = jnp.where(left > 0.0, m, kth)
        left = left - jnp.sum(jnp.where(hit, 1.0, 0.0), axis=0, keepdims=True)
        work = jnp.where(hit, -jnp.inf, work)
    return kth


def _staircase_candidates(v1, v2):
    k = PEER_TOPK + 1
    sub = 8
    v1m = jnp.concatenate(v1[:2 * sub], axis=0)
    v2m = jnp.concatenate(v2[:2 * sub], axis=0)
    row = lax.broadcasted_iota(jnp.int32, (sub, v2m.shape[1]), 0)
    blocks = [v1[0] + v2m]
    for i in range(1, sub):
        keep = k // (i + 1)
        blk = v1[i] + v2m[0:sub]
        blocks.append(blk if keep >= sub else jnp.where(row < keep, blk, -jnp.inf))
    blocks.append(v1m[sub:] + v2[0])
    pad = jnp.full((sub - 2, v2m.shape[1]), -jnp.inf, F32)
    blocks.append(jnp.concatenate([v1[2 * sub] + v2[0], v1[0] + v2[2 * sub], pad], axis=0))
    return jnp.concatenate(blocks, axis=0)


def _peer_topk_kernel(qt_ref, k1_ref, k2_ref, s1_ref, s2_ref, st_ref, *, n_heads, n_keys):
    half = k1_ref.shape[1]
    hp = lax.Precision.HIGHEST
    k1 = k1_ref[...]
    k2 = k2_ref[...]
    taus = []
    for h in range(n_heads):
        base = h * 2 * half
        s1 = jnp.dot(k1, qt_ref[base:base + half, :], preferred_element_type=F32, precision=hp)
        s2 = jnp.dot(k2, qt_ref[base + half:base + 2 * half, :], preferred_element_type=F32,
                     precision=hp)
        a1 = s1 * LOG2E
        a2 = s2 * LOG2E
        v1 = _kth_largest_rows(a1, PEER_TOPK + 1)
        v2 = _kth_largest_rows(a2, PEER_TOPK + 1)
        v1s = [v - v1[0] for v in v1]
        v2s = [v - v2[0] for v in v2]
        cand = _staircase_candidates(v1s, v2s)
        picked = cand >= _kth_largest_value(cand, PEER_TOPK)
        zsum = jnp.sum(jnp.where(picked, jnp.exp2(cand), 0.0), axis=0, keepdims=True)
        shift1 = v1[0] + jnp.log2(zsum) + 1.0
        cand_z = _staircase_candidates([v - shift1 for v in v1], v2s)
        last_in = jnp.min(jnp.where(picked, cand_z, jnp.inf), axis=0, keepdims=True)
        first_out = jnp.max(jnp.where(picked, -jnp.inf, cand_z), axis=0, keepdims=True)
        taus.append(0.5 * (last_in + first_out))
        s1_ref[h * n_keys:(h + 1) * n_keys, :] = a1 - shift1
        s2_ref[h * n_keys:(h + 1) * n_keys, :] = a2 - v2[0]
    st_ref[...] = jnp.concatenate(taus, axis=0)


def _peer_topk(qt, k1, k2, tn=256):
    hq, t = qt.shape
    n_keys, half = k1.shape
    n_heads = hq // (2 * half)
    srows = n_heads * n_keys
    col = lambda rows: pl.BlockSpec((rows, tn), lambda i: (0, i))
    return pl.pallas_call(
        functools.partial(_peer_topk_kernel, n_heads=n_heads, n_keys=n_keys),
        grid=(t // tn,),
        in_specs=[col(hq),
                  pl.BlockSpec((n_keys, half), lambda i: (0, 0)),
                  pl.BlockSpec((n_keys, half), lambda i: (0, 0))],
        out_specs=[col(srows), col(srows), col(n_heads)],
        out_shape=[jax.ShapeDtypeStruct((srows, t), F32), jax.ShapeDtypeStruct((srows, t), F32),
                   jax.ShapeDtypeStruct((n_heads, t), F32)],
        compiler_params=_params(("parallel",)),
        name="peer_topk",
    )(qt, k1.astype(F32), k2.astype(F32))


def _peer_dense_kernel(h_ref, u_ref, v_ref, l1_ref, l2_ref, tau_ref, o_ref, tau8, l1b, wt, out_acc,
                       *, n_heads, n_keys):
    j = pl.program_id(1)
    te = v_ref.shape[0]
    tm = h_ref.shape[0]
    d = v_ref.shape[1]
    na = te // n_keys
    sub = 8
    dc = 1024

    @pl.when(j == 0)
    def _():
        out_acc[...] = jnp.zeros_like(out_acc)
        for h in range(n_heads):
            tau8[h * sub:(h + 1) * sub, :] = jnp.broadcast_to(tau_ref[h:h + 1, :], (sub, tm))

    for aa in range(na):
        for h in range(n_heads):
            row = l1_ref[pl.ds(h * n_keys + j * na + aa, 1), :]
            k = (aa * n_heads + h) * sub
            l1b[k:k + sub, :] = jnp.broadcast_to(row, (sub, tm))

    ax = 2
    link = jnp.zeros((sub, tm), F32)
    for a0 in range(0, na, ax):
        for b0 in range(0, n_keys, 2 * sub):
            acc = [[link, link] for _ in range(ax)]
            for h in range(n_heads):
                t8 = tau8[h * sub:(h + 1) * sub, :]
                r2 = [slice(h * n_keys + b0 + y * sub, h * n_keys + b0 + (y + 1) * sub) for y in range(2)]
                l2v = [l2_ref[r, :] for r in r2]
                for x in range(ax):
                    k = ((a0 + x) * n_heads + h) * sub
                    l1v = l1b[k:k + sub, :]
                    for y in range(2):
                        logit = l1v + l2v[y]
                        acc[x][y] = acc[x][y] + jnp.exp2(jnp.where(logit >= t8, logit, -jnp.inf))
            for x in range(ax):
                r0 = (a0 + x) * n_keys + b0
                wt[r0:r0 + 2 * sub, :] = jnp.concatenate(acc[x], axis=0).astype(wt.dtype)
            link = jnp.minimum(pltpu.roll(acc[0][0], 1, 1), 0.0)

    w = wt[...].T
    act = lax.dot_general(h_ref[...], u_ref[...], (((1,), (1,)), ((), ())),
                          preferred_element_type=F32)
    inner = act * (GELU_C0 + GELU_C1 * (act * act))
    g = (act * (1.0 + jnp.tanh(inner))).astype(BF16) * w
    for c0 in range(0, d, dc):
        out_acc[:, c0:c0 + dc] += jnp.dot(g, v_ref[:, c0:c0 + dc], preferred_element_type=F32)

    @pl.when(j == pl.num_programs(1) - 1)
    def _():
        o_ref[...] = out_acc[...].astype(o_ref.dtype)


def _peer_dense(h2, u_tab, v_tab, layer, s1t, s2t, stats, tm=512, te=512):
    t, d = h2.shape
    ne = v_tab.shape[1]
    n_keys = int(round(math.sqrt(ne)))
    n_heads = s1t.shape[0] // n_keys
    srows = s1t.shape[0]
    na = te // n_keys
    tok = lambda rows: pl.BlockSpec((rows, tm), lambda i, j: (0, i))
    return pl.pallas_call(
        functools.partial(_peer_dense_kernel, n_heads=n_heads, n_keys=n_keys),
        grid=(t // tm, ne // te),
        in_specs=[pl.BlockSpec((tm, d), lambda i, j: (i, 0)),
                  pl.BlockSpec((None, te, d), lambda i, j: (layer, j, 0)),
                  pl.BlockSpec((None, te, d), lambda i, j: (layer, j, 0)),
                  tok(srows), tok(srows), tok(n_heads)],
        out_specs=pl.BlockSpec((tm, d), lambda i, j: (i, 0)),
        out_shape=jax.ShapeDtypeStruct((t, d), BF16),
        scratch_shapes=[pltpu.VMEM((n_heads * 8, tm), F32),
                        pltpu.VMEM((na * n_heads * 8, tm), F32),
                        pltpu.VMEM((te, tm), BF16),
                        pltpu.VMEM((tm, d), F32)],
        compiler_params=_params(("parallel", "arbitrary")),
        name="peer_dense",
    )(h2, u_tab, v_tab, s1t, s2t, stats)


def kernel(x, c, w_ada, b_ada, ada_layer, norm1_g, norm2_g, w_in, lam_re, lam_im, log_dt, b_re, b_im,
           c_re, c_im, d_skip, w_glu, q_gain, k_gain, sinks, gn_ssm, gn_attn, w_out, peer_wq, peer_k1,
           peer_k2, peer_u, peer_v):
    bsz, seq, d = x.shape
    t = bsz * seq
    depth = w_in.shape[0]
    n_mod = ada_layer.shape[1]
    ssm_w = w_glu.shape[1]
    attn_w = gn_attn.shape[1]
    kv_w = (w_in.shape[2] - ssm_w - attn_w) // 2

    w_in_b, w_glu_b, w_out_b = w_in, w_glu.astype(BF16), w_out
    wq_t_b = peer_wq.transpose(0, 2, 1).astype(BF16)
    u_b, v_b = peer_u.astype(BF16), peer_v.astype(BF16)

    s5_ops = jax.vmap(_s5_operators)(lam_re, lam_im, log_dt, b_re, b_im, c_re, c_im, d_skip)

    cond = _cond(c, w_ada, b_ada).reshape(bsz, n_mod, d)
    xf = x.astype(F32).reshape(t, d)
    delta, gate_prev = None, None
    for l in range(depth):
        mod = cond + ada_layer[l].astype(F32)
        shift1, scale1, gate1, shift2, scale2, gate2 = (mod[:, i] for i in range(n_mod))

        xf, h = _adaln_norm(xf, delta, gate_prev, norm1_g[l], scale1, shift1, seq)
        z_ssm = _matmul_channel_blocks(h, w_in_b, l, ssm_w, BF16, tm=1024, tn=512, steps=S5_CHUNK)
        z_attn = _matmul(h, w_in_b, l, ssm_w, attn_w + 2 * kv_w, BF16, tm=1024, tn=512)
        g = _s5_core(z_ssm, s5_ops, l, bsz)
        y_ssm = _glu_norm(g, w_glu_b, l, gn_ssm[l], steps=S5_CHUNK)
        y_attn = _attention(z_attn, 0, attn_w, kv_w, q_gain[l], k_gain[l], sinks[l], gn_attn[l], bsz)
        mixed = _matmul_concat(y_ssm, y_attn, w_out_b, l, BF16, tm=1024, tn=512)

        xf, h2 = _adaln_norm(xf, mixed, gate1, norm2_g[l], scale2, shift2, seq)
        qt = _matmul_nt(wq_t_b, l, h2, F32, tn=512)
        s1t, s2t, stats = _peer_topk(qt, peer_k1[l], peer_k2[l])
        delta = _peer_dense(h2, u_b, v_b, l, s1t, s2t, stats)
        gate_prev = gate2
    out = _residual_add(xf, delta, gate_prev, seq)
    return out.reshape(bsz, seq, d).astype(x.dtype)
```

```python
import functools
import math

import jax
import jax.numpy as jnp
from jax import lax
from jax.experimental import pallas as pl
from jax.experimental.pallas import tpu as pltpu

F32 = jnp.float32
BF16 = jnp.bfloat16
EPS = 1e-6
WINDOW = 128
PEER_TOPK = 16
LOG2E = 1.4426950408889634
GELU_C0 = math.sqrt(2.0 / math.pi)
GELU_C1 = 0.044715 * GELU_C0
S5_CHUNK = 16
S5_GROUP_BLOCK = 8
V7X_VMEM_LIMIT = 56 * 1024 * 1024


def _params(semantics, vmem=V7X_VMEM_LIMIT, flags=None):
    return pltpu.CompilerParams(dimension_semantics=semantics, vmem_limit_bytes=vmem, flags=flags)


def _mm_kernel(a_ref, b_ref, o_ref):
    o_ref[...] = jnp.dot(a_ref[...], b_ref[...].astype(a_ref.dtype),
                         preferred_element_type=F32).astype(o_ref.dtype)


def _matmul(a, b, layer, col0, n, out_dtype, tm, tn):
    m, k = a.shape
    j0 = col0 // tn
    return pl.pallas_call(
        _mm_kernel,
        grid=(m // tm, n // tn),
        in_specs=[pl.BlockSpec((tm, k), lambda i, j: (i, 0)),
                  pl.BlockSpec((None, k, tn), lambda i, j: (layer, 0, j0 + j))],
        out_specs=pl.BlockSpec((tm, tn), lambda i, j: (i, j)),
        out_shape=jax.ShapeDtypeStruct((m, n), out_dtype),
        compiler_params=_params(("parallel", "parallel")),
        name="matmul",
    )(a, b)


def _mm_cb_kernel(a_ref, b_ref, o_ref, scr):
    res = jnp.dot(a_ref[...], b_ref[...].astype(a_ref.dtype), preferred_element_type=F32)
    nb, rows, width = o_ref.shape
    lanes = scr.shape[2]
    steps = width // lanes
    for k in range(nb):
        scr[k] = res[:, k * lanes:(k + 1) * lanes]
    for k in range(nb):
        for s in range(steps):
            o_ref[k, :, s * lanes:(s + 1) * lanes] = (
                scr[k, pl.ds(s, rows, stride=steps), :].astype(o_ref.dtype))


def _matmul_channel_blocks(a, b, layer, n, out_dtype, tm, tn, steps, lanes=128):
    m, k = a.shape
    return pl.pallas_call(
        _mm_cb_kernel,
        grid=(m // tm, n // tn),
        in_specs=[pl.BlockSpec((tm, k), lambda i, j: (i, 0)),
                  pl.BlockSpec((None, k, tn), lambda i, j: (layer, 0, j))],
        out_specs=pl.BlockSpec((tn // lanes, tm // steps, steps * lanes), lambda i, j: (j, i, 0)),
        out_shape=jax.ShapeDtypeStruct((n // lanes, m // steps, steps * lanes), out_dtype),
        scratch_shapes=[pltpu.VMEM((tn // lanes, tm, lanes), F32)],
        compiler_params=_params(("parallel", "parallel")),
        name="matmul_channel_blocks",
    )(a, b)


def _mm2_kernel(a1_ref, a2_ref, b1_ref, b2_ref, o_ref):
    acc = jnp.dot(a1_ref[...], b1_ref[...].astype(a1_ref.dtype), preferred_element_type=F32)
    acc += jnp.dot(a2_ref[...], b2_ref[...].astype(a2_ref.dtype), preferred_element_type=F32)
    o_ref[...] = acc.astype(o_ref.dtype)


def _matmul_concat(a1, a2, b, layer, out_dtype, tm, tn):
    m, k1 = a1.shape
    k2 = a2.shape[1]
    assert k1 == k2
    n = b.shape[2]
    return pl.pallas_call(
        _mm2_kernel,
        grid=(m // tm, n // tn),
        in_specs=[pl.BlockSpec((tm, k1), lambda i, j: (i, 0)),
                  pl.BlockSpec((tm, k2), lambda i, j: (i, 0)),
                  pl.BlockSpec((None, k1, tn), lambda i, j: (layer, 0, j)),
                  pl.BlockSpec((None, k2, tn), lambda i, j: (layer, 1, j))],
        out_specs=pl.BlockSpec((tm, tn), lambda i, j: (i, j)),
        out_shape=jax.ShapeDtypeStruct((m, n), out_dtype),
        compiler_params=_params(("parallel", "parallel")),
        name="matmul_concat",
    )(a1, a2, b, b)


def _mm_nt_kernel(a_ref, b_ref, o_ref):
    o_ref[...] = lax.dot_general(a_ref[...], b_ref[...], (((1,), (1,)), ((), ())),
                                 preferred_element_type=F32).astype(o_ref.dtype)


def _matmul_nt(a, layer, b, out_dtype, tn):
    _, m, k = a.shape
    n = b.shape[0]
    return pl.pallas_call(
        _mm_nt_kernel,
        grid=(n // tn,),
        in_specs=[pl.BlockSpec((None, m, k), lambda j: (layer, 0, 0)),
                  pl.BlockSpec((tn, k), lambda j: (j, 0))],
        out_specs=pl.BlockSpec((m, tn), lambda j: (0, j)),
        out_shape=jax.ShapeDtypeStruct((m, n), out_dtype),
        compiler_params=_params(("parallel",)),
        name="matmul_nt",
    )(a, b)


def _cond_kernel(c_ref, w_ref, b_ref, o_ref):
    c = c_ref[...]
    s = c * jax.nn.sigmoid(c)
    o_ref[...] = jnp.dot(s, w_ref[...], preferred_element_type=F32,
                         precision=lax.Precision.HIGHEST) + b_ref[...]


def _cond(c, w_ada, b_ada, tn=512):
    bsz, d = c.shape
    n = w_ada.shape[1]
    rows = 8
    cp = jnp.zeros((rows, d), F32).at[:bsz].set(c.astype(F32))
    out = pl.pallas_call(
        _cond_kernel,
        grid=(n // tn,),
        in_specs=[pl.BlockSpec((rows, d), lambda j: (0, 0)),
                  pl.BlockSpec((d, tn), lambda j: (0, j)),
                  pl.BlockSpec((1, tn), lambda j: (0, j))],
        out_specs=pl.BlockSpec((rows, tn), lambda j: (0, j)),
        out_shape=jax.ShapeDtypeStruct((rows, n), F32),
        compiler_params=_params(("parallel",)),
        name="adaln_cond",
    )(cp, w_ada.astype(F32), b_ada.astype(F32).reshape(1, n))
    return out[:bsz]


def _norm_body(x, g_ref, scale_ref, shift_ref, h_ref):
    y = x * lax.rsqrt(jnp.mean(x * x, axis=-1, keepdims=True) + EPS)
    y = y * g_ref[...]
    h_ref[...] = (y * (1.0 + scale_ref[0]) + shift_ref[0]).astype(h_ref.dtype)


def _norm_kernel(x_ref, g_ref, scale_ref, shift_ref, h_ref):
    _norm_body(x_ref[...], g_ref, scale_ref, shift_ref, h_ref)


def _resnorm_kernel(x_ref, d_ref, gate_ref, g_ref, scale_ref, shift_ref, xo_ref, h_ref):
    x = x_ref[...] + gate_ref[0] * d_ref[...]
    xo_ref[...] = x
    _norm_body(x, g_ref, scale_ref, shift_ref, h_ref)


def _adaln_norm(x, delta, gate, g, scale, shift, rows_per_batch, tm=256):
    t, d = x.shape
    tpb = rows_per_batch // tm
    row = pl.BlockSpec((tm, d), lambda i: (i, 0))
    per_batch = pl.BlockSpec((1, 1, d), lambda i: (i // tpb, 0, 0))
    vec = pl.BlockSpec((1, d), lambda i: (0, 0))
    g2 = g.astype(F32).reshape(1, d)
    b3 = lambda v: v.astype(F32).reshape(v.shape[0], 1, d)
    if delta is None:
        h = pl.pallas_call(
            _norm_kernel,
            grid=(t // tm,),
            in_specs=[row, vec, per_batch, per_batch],
            out_specs=row,
            out_shape=jax.ShapeDtypeStruct((t, d), BF16),
            compiler_params=_params(("parallel",)),
            name="adaln_norm",
        )(x, g2, b3(scale), b3(shift))
        return x, h
    return pl.pallas_call(
        _resnorm_kernel,
        grid=(t // tm,),
        in_specs=[row, row, per_batch, vec, per_batch, per_batch],
        out_specs=[row, row],
        out_shape=[jax.ShapeDtypeStruct((t, d), F32), jax.ShapeDtypeStruct((t, d), BF16)],
        compiler_params=_params(("parallel",)),
        name="residual_adaln_norm",
    )(x, delta, b3(gate), g2, b3(scale), b3(shift))


def _residual_kernel(x_ref, d_ref, gate_ref, o_ref):
    o_ref[...] = x_ref[...] + gate_ref[0] * d_ref[...]


def _residual_add(x, delta, gate, rows_per_batch, tm=256):
    t, d = x.shape
    tpb = rows_per_batch // tm
    row = pl.BlockSpec((tm, d), lambda i: (i, 0))
    return pl.pallas_call(
        _residual_kernel,
        grid=(t // tm,),
        in_specs=[row, row, pl.BlockSpec((1, 1, d), lambda i: (i // tpb, 0, 0))],
        out_specs=row,
        out_shape=jax.ShapeDtypeStruct((t, d), F32),
        compiler_params=_params(("parallel",)),
        name="residual_add",
    )(x, delta, gate.astype(F32).reshape(gate.shape[0], 1, d))


def _s5_operators(lam_re, lam_im, log_dt, b_re, b_im, c_re, c_im, d_skip):
    hp = lax.Precision.HIGHEST
    j = S5_CHUNK
    f = lambda v: v.astype(F32)
    lr, li = f(lam_re), f(lam_im)
    g, p = lr.shape
    h = b_re.shape[-1]
    dt = jnp.exp(f(log_dt))[:, None]
    steps = jnp.arange(j + 1, dtype=F32)[:, None, None]
    pmag = jnp.exp(steps * (lr * dt))
    pr = pmag * jnp.cos(steps * (li * dt))
    pi = pmag * jnp.sin(steps * (li * dt))
    ar, ai = pr[1], pi[1]
    den = lr * lr + li * li
    kr = ((ar - 1.0) * lr + ai * li) / den
    ki = (ai * lr - (ar - 1.0) * li) / den
    br, bi = f(b_re), f(b_im)
    bbr = kr[..., None] * br - ki[..., None] * bi
    bbi = kr[..., None] * bi + ki[..., None] * br
    cr, ci = f(c_re), f(c_im)
    er = pr[:j, :, :, None] * bbr - pi[:j, :, :, None] * bbi
    ei = pr[:j, :, :, None] * bbi + pi[:j, :, :, None] * bbr
    kk = (jnp.einsum('ghp,ngpk->nghk', cr, er, precision=hp)
          - jnp.einsum('ghp,ngpk->nghk', ci, ei, precision=hp))
    kk = kk.at[0].add(f(d_skip)[:, :, None] * jnp.eye(h, dtype=F32))
    gb = S5_GROUP_BLOCK
    nblk = g // gb
    t_op = kk.reshape(j, nblk, gb, h, h).transpose(1, 0, 4, 2, 3).reshape(nblk, j, h, gb * h)

    prr = pr[j - 1::-1].reshape(j, nblk, gb, p).transpose(1, 0, 2, 3)[:, :, None]
    pir = pi[j - 1::-1].reshape(j, nblk, gb, p).transpose(1, 0, 2, 3)[:, :, None]
    bbr_e = bbr.reshape(nblk, gb, p, h).transpose(0, 3, 1, 2)[:, None]
    bbi_e = bbi.reshape(nblk, gb, p, h).transpose(0, 3, 1, 2)[:, None]
    bc_r = (prr * bbr_e - pir * bbi_e).reshape(nblk, j * h, gb * p)
    bc_i = (prr * bbi_e + pir * bbr_e).reshape(nblk, j * h, gb * p)

    cr_e = cr.reshape(nblk, gb, h, p).transpose(0, 3, 1, 2)[:, :, None]
    ci_e = ci.reshape(nblk, gb, h, p).transpose(0, 3, 1, 2)[:, :, None]
    p1r = pr[1:].reshape(j, nblk, gb, p).transpose(1, 3, 0, 2)[..., None]
    p1i = pi[1:].reshape(j, nblk, gb, p).transpose(1, 3, 0, 2)[..., None]
    cc_r = (cr_e * p1r - ci_e * p1i).reshape(nblk, p, j * gb * h)
    cc_i = (-(cr_e * p1i + ci_e * p1r)).reshape(nblk, p, j * gb * h)

    a16r = pr[j].reshape(nblk, 1, gb * p)
    a16i = pi[j].reshape(nblk, 1, gb * p)
    return (t_op.astype(BF16), bc_r.astype(BF16), bc_i.astype(BF16),
            cc_r.astype(BF16), cc_i.astype(BF16), a16r, a16i)


def _s5_kernel(u_ref, t_ref, bre_ref, bim_ref, cre_ref, cim_ref, ar_ref, ai_ref, mt_ref, mb_ref,
               o_ref, tblk, bblk_r, bblk_i, cblk_r, cblk_i, zre, zim, pre, pim,
               *, chunks_per_batch, n_batch):
    gb = S5_GROUP_BLOCK
    j = S5_CHUNK
    hh = t_ref.shape[2]
    pp = cre_ref.shape[1]
    lanes = gb * hh
    ncol = 4 * lanes
    for i in range(j):
        j_first = (i * lanes // ncol) * ncol // lanes
        for gl in range(gb):
            src = slice(i * hh, (i + 1) * hh)
            dst = slice(i * lanes + gl * hh, i * lanes + (gl + 1) * hh)
            m_gl = mt_ref[gl, :, 0:lanes]
            for jj in range(j_first, j):
                blk = t_ref[0, jj - i] * m_gl if jj >= i else jnp.zeros((hh, lanes), tblk.dtype)
                tblk[dst, jj * lanes:(jj + 1) * lanes] = blk
            bblk_r[dst, :] = bre_ref[0, src, :] * mb_ref[gl]
            bblk_i[dst, :] = bim_ref[0, src, :] * mb_ref[gl]
    for gl in range(gb):
        for r0 in range(0, pp, hh):
            src = slice(r0, r0 + hh)
            dst = slice(gl * pp + r0, gl * pp + r0 + hh)
            cblk_r[dst, :] = cre_ref[0, src, :] * mt_ref[gl]
            cblk_i[dst, :] = cim_ref[0, src, :] * mt_ref[gl]

    u = u_ref[0]
    zre[...] = jnp.dot(u, bblk_r[...], preferred_element_type=F32)
    zim[...] = jnp.dot(u, bblk_i[...], preferred_element_type=F32)
    ar = ar_ref[0]
    ai = ai_ref[0]

    def step(c, carry):
        new = []
        for b in range(n_batch):
            sr, si = carry[2 * b], carry[2 * b + 1]
            row = pl.ds(b * chunks_per_batch + c, 1)
            pre[row, :] = sr
            pim[row, :] = si
            new.append(ar * sr - ai * si + zre[row, :])
            new.append(ar * si + ai * sr + zim[row, :])
        return tuple(new)

    zero = jnp.zeros((1, zre.shape[1]), F32)
    lax.fori_loop(0, chunks_per_batch, step, (zero,) * (2 * n_batch), unroll=8)

    pr_b = pre[...].astype(BF16)
    pi_b = pim[...].astype(BF16)
    for c0 in range(0, j * lanes, ncol):
        cols = slice(c0, c0 + ncol)
        k_rows = c0 + ncol
        y = jnp.dot(u[:, :k_rows], tblk[:k_rows, cols], preferred_element_type=F32)
        y += jnp.dot(pr_b, cblk_r[:, cols], preferred_element_type=F32)
        y += jnp.dot(pi_b, cblk_i[:, cols], preferred_element_type=F32)
        o_ref[0, :, cols] = jax.nn.gelu(y).astype(o_ref.dtype)


def _s5_core(uc, ops, layer, n_batch):
    t_op, bre, bim, cre, cim, a16r, a16i = ops
    j = S5_CHUNK
    gb = S5_GROUP_BLOCK
    nblk, nchunk, width = uc.shape
    lanes = width // j
    hh = lanes // gb
    pp = cre.shape[2]
    def own(width, per_group):
        col_group = jnp.arange(width) // per_group % gb
        m = col_group[None, None, :] == jnp.arange(gb)[:, None, None]
        return jnp.broadcast_to(m, (gb, hh, width)).astype(BF16)

    mask_t = own(j * lanes, hh)
    mask_b = own(gb * pp, pp)
    blk = lambda r, c: pl.BlockSpec((1, r, c), lambda i: (i, 0, 0))
    lblk = lambda r, c: pl.BlockSpec((None, 1, r, c), lambda i: (layer, i, 0, 0))
    const = lambda a: pl.BlockSpec(a.shape, lambda i: (0, 0, 0))
    yc = pl.pallas_call(
        functools.partial(_s5_kernel, chunks_per_batch=nchunk // n_batch, n_batch=n_batch),
        grid=(nblk,),
        in_specs=[blk(nchunk, j * lanes),
                  pl.BlockSpec((None, 1, j, hh, lanes), lambda i: (layer, i, 0, 0, 0)),
                  lblk(j * hh, gb * pp), lblk(j * hh, gb * pp),
                  lblk(pp, j * lanes), lblk(pp, j * lanes),
                  lblk(1, gb * pp), lblk(1, gb * pp), const(mask_t), const(mask_b)],
        out_specs=blk(nchunk, j * lanes),
        out_shape=jax.ShapeDtypeStruct((nblk, nchunk, j * lanes), BF16),
        scratch_shapes=[pltpu.VMEM((j * lanes, j * lanes), BF16),
                        pltpu.VMEM((j * lanes, gb * pp), BF16), pltpu.VMEM((j * lanes, gb * pp), BF16),
                        pltpu.VMEM((gb * pp, j * lanes), BF16), pltpu.VMEM((gb * pp, j * lanes), BF16)]
                       + [pltpu.VMEM((nchunk, gb * pp), F32)] * 4,
        compiler_params=_params(("parallel",)),
        name="s5_chunked_scan",
    )(uc, t_op, bre, bim, cre, cim, a16r, a16i, mask_t, mask_b)
    return yc


def _glu_norm_kernel(g_ref, w_ref, gn_ref, o_ref, scr):
    nblk, rows, width = g_ref.shape
    lanes = scr.shape[2]
    steps = width // lanes
    for cb in range(nblk):
        for s in range(steps):
            scr[cb, pl.ds(s, rows, stride=steps), :] = (
                g_ref[cb, :, s * lanes:(s + 1) * lanes].astype(F32))
    g32 = jnp.concatenate([scr[cb] for cb in range(nblk)], axis=1)
    gate = jax.nn.sigmoid(jnp.dot(g32.astype(BF16), w_ref[...], preferred_element_type=F32))
    y = g32 * gate
    y = y * lax.rsqrt(jnp.mean(y * y, axis=-1, keepdims=True) + EPS)
    o_ref[...] = (y * gn_ref[...]).astype(o_ref.dtype)


def _glu_norm(gc, w_glu, layer, gn, steps, tm=512, lanes=128):
    nblk, nrow, width = gc.shape
    t = nrow * steps
    w = nblk * lanes
    return pl.pallas_call(
        _glu_norm_kernel,
        grid=(t // tm,),
        in_specs=[pl.BlockSpec((nblk, tm // steps, width), lambda i: (0, i, 0)),
                  pl.BlockSpec((None, w, w), lambda i: (layer, 0, 0)),
                  pl.BlockSpec((1, w), lambda i: (0, 0))],
        out_specs=pl.BlockSpec((tm, w), lambda i: (i, 0)),
        out_shape=jax.ShapeDtypeStruct((t, w), BF16),
        scratch_shapes=[pltpu.VMEM((nblk, tm, lanes), F32)],
        compiler_params=_params(("parallel",)),
        name="glu_groupnorm",
    )(gc, w_glu, gn.astype(F32).reshape(1, w))


def _head_rms_scale(x, sel_ref, exp_ref, head_dim):
    ss = jnp.dot((x * x).astype(BF16), sel_ref[...], preferred_element_type=F32)
    r = lax.rsqrt(ss * (1.0 / head_dim) + EPS)
    r_hi = r.astype(BF16)
    r_lo = (r - r_hi.astype(F32)).astype(BF16)
    return (jnp.dot(r_hi, exp_ref[...], preferred_element_type=F32)
            + jnp.dot(r_lo, exp_ref[...], preferred_element_type=F32))


def _attn_kernel(sink_ref, q_ref, kp_ref, kc_ref, vp_ref, vc_ref, qg_ref, kg_ref, gn_ref,
                 qsel_ref, qexp_ref, ksel_ref, kexp_ref, o_ref, acc,
                 *, n_q_heads, n_kv_heads, head_dim):
    nblk = pl.program_id(1)
    w = WINDOW
    lw = 2 * head_dim
    grp = n_q_heads // n_kv_heads
    t_loc = lax.broadcasted_iota(jnp.int32, (w, 2 * w), 0)
    s_loc = lax.broadcasted_iota(jnp.int32, (w, 2 * w), 1)
    dist = t_loc + w - s_loc
    valid = (dist >= 0) & (dist < w) & (s_loc + nblk * w >= w)
    dmask = jnp.where(valid, dist.astype(F32), jnp.inf)

    q = q_ref[...].astype(F32)
    qn = (q * _head_rms_scale(q, qsel_ref, qexp_ref, head_dim) * qg_ref[...]).astype(BF16)
    k = jnp.concatenate([kp_ref[...], kc_ref[...]], axis=0).astype(F32)
    kn = k * _head_rms_scale(k, ksel_ref, kexp_ref, head_dim) * kg_ref[...]
    v = jnp.concatenate([vp_ref[...], vc_ref[...]], axis=0).astype(F32)

    lane = lax.broadcasted_iota(jnp.int32, (2 * w, lw), 1)
    lo = lane < head_dim

    def block_diag(x, hk):
        grp_lanes = x[:, (hk // 2) * lw:(hk // 2 + 1) * lw]
        swapped = pltpu.roll(grp_lanes, head_dim, 1)
        low, high = (grp_lanes, swapped) if hk % 2 == 0 else (swapped, grp_lanes)
        return jnp.concatenate([jnp.where(lo, low, 0.0), jnp.where(lo, 0.0, high)],
                               axis=0).astype(BF16)

    lane_q = lax.broadcasted_iota(jnp.int32, (w, lw), 1) < head_dim
    for hk in range(n_kv_heads):
        kb = block_diag(kn, hk)
        vb = block_diag(v, hk)
        for pq in range(grp // 2):
            pair = hk * (grp // 2) + pq
            cols = slice(pair * lw, (pair + 1) * lw)
            s = lax.dot_general(qn[:, cols], kb, (((1,), (1,)), ((), ())),
                                preferred_element_type=F32)
            es, inv = [], []
            for i in range(2):
                hq = 2 * pair + i
                slope = 2.0 ** (-8.0 * (hq + 1) / n_q_heads)
                si = s[:, i * 2 * w:(i + 1) * 2 * w] - slope * dmask
                sink = sink_ref[hq]
                m = jnp.maximum(jnp.max(si, axis=-1, keepdims=True), sink)
                e = jnp.exp(si - m)
                inv.append(1.0 / (jnp.sum(e, axis=-1, keepdims=True) + jnp.exp(sink - m)))
                es.append(e.astype(BF16))
            pv = jnp.dot(jnp.concatenate(es, axis=1), vb, preferred_element_type=F32)
            acc[:, cols] = pv * jnp.where(lane_q, inv[0], inv[1])
    y = acc[...]
    y = y * lax.rsqrt(jnp.mean(y * y, axis=-1, keepdims=True) + EPS)
    o_ref[...] = (y * gn_ref[...]).astype(o_ref.dtype)


def _attention(z, ssm_w, attn_w, kv_w, q_gain, k_gain, sinks, gn, n_batch):
    t = z.shape[0]
    w = WINDOW
    head_dim = q_gain.shape[0]
    n_q = attn_w // head_dim
    n_kv = kv_w // head_dim
    nb = t // n_batch // w
    qcol = ssm_w // attn_w
    kcol = (ssm_w + attn_w) // kv_w
    vcol = kcol + 1
    cur = lambda col: (lambda b, n: (b * nb + n, col))
    prev = lambda col: (lambda b, n: (b * nb + jnp.maximum(n - 1, 0), col))
    full = lambda r, c: pl.BlockSpec((r, c), lambda b, n: (0, 0))
    nsel = 128

    def selectors(width):
        sel = (jnp.arange(width)[:, None] // head_dim == jnp.arange(nsel)[None, :]).astype(BF16)
        return sel, sel.T

    qsel, qexp = selectors(attn_w)
    ksel, kexp = selectors(kv_w)
    qg_row = (jnp.tile(q_gain.astype(F32), n_q) * head_dim ** -0.5).reshape(1, attn_w)
    kg_row = jnp.tile(k_gain.astype(F32), n_kv).reshape(1, kv_w)
    return pl.pallas_call(
        functools.partial(_attn_kernel, n_q_heads=n_q, n_kv_heads=n_kv, head_dim=head_dim),
        grid=(n_batch, nb),
        in_specs=[pl.BlockSpec(memory_space=pltpu.SMEM),
                  pl.BlockSpec((w, attn_w), cur(qcol)),
                  pl.BlockSpec((w, kv_w), prev(kcol)),
                  pl.BlockSpec((w, kv_w), cur(kcol)),
                  pl.BlockSpec((w, kv_w), prev(vcol)),
                  pl.BlockSpec((w, kv_w), cur(vcol)),
                  full(1, attn_w), full(1, kv_w), full(1, attn_w),
                  full(attn_w, nsel), full(nsel, attn_w), full(kv_w, nsel), full(nsel, kv_w)],
        out_specs=pl.BlockSpec((w, attn_w), lambda b, n: (b * nb + n, 0)),
        out_shape=jax.ShapeDtypeStruct((t, attn_w), BF16),
        scratch_shapes=[pltpu.VMEM((w, attn_w), F32)],
        compiler_params=_params(("parallel", "parallel")),
        name="swa_attention",
    )(sinks.astype(F32), z, z, z, z, z, qg_row, kg_row, gn.astype(F32).reshape(1, attn_w),
      qsel, qexp, ksel, kexp)


def _sorting_network(n):
    pairs = []
    p = 1
    while p < n:
        k = p
        while k >= 1:
            for j in range(k % p, n - k, 2 * k):
                for i in range(min(k, n - j - k)):
                    if (i + j) // (2 * p) == (i + j + k) // (2 * p):
                        pairs.append((i + j, i + j + k))
            k //= 2
        p *= 2
    return pairs


def _kth_largest_rows(vals, k):
    sub = 8
    tiles = [vals[r:r + sub, :] for r in range(0, vals.shape[0], sub)]
    for i, j in _sorting_network(len(tiles)):
        hi = jnp.maximum(tiles[i], tiles[j])
        tiles[j] = jnp.minimum(tiles[i], tiles[j])
        tiles[i] = hi
    out = []
    for it in range(k):
        m = jnp.max(tiles[0], axis=0, keepdims=True)
        out.append(m)
        hit = tiles[0] == m
        depth = min(len(tiles), k - it)
        for r in range(depth - 1):
            tiles[r] = jnp.where(hit, tiles[r + 1], tiles[r])
        if depth == len(tiles):
            tiles[depth - 1] = jnp.where(hit, -jnp.inf, tiles[depth - 1])
    return out


def _kth_largest_value(vals, k):
    work = vals
    left = jnp.full((1, vals.shape[1]), float(k), F32)
    kth = jnp.full((1, vals.shape[1]), -jnp.inf, F32)
    for _ in range(k):
        m = jnp.max(work, axis=0, keepdims=True)
        hit = work == m
        kth = jnp.where(left > 0.0, m, kth)
        left = left - jnp.sum(jnp.where(hit, 1.0, 0.0), axis=0, keepdims=True)
        work = jnp.where(hit, -jnp.inf, work)
    return kth


def _staircase_candidates(v1, v2):
    k = PEER_TOPK + 1
    sub = 8
    v1m = jnp.concatenate(v1[:2 * sub], axis=0)
    v2m = jnp.concatenate(v2[:2 * sub], axis=0)
    row = lax.broadcasted_iota(jnp.int32, (sub, v2m.shape[1]), 0)
    blocks = [v1[0] + v2m]
    for i in range(1, sub):
        keep = k // (i + 1)
        blk = v1[i] + v2m[0:sub]
        blocks.append(blk if keep >= sub else jnp.where(row < keep, blk, -jnp.inf))
    blocks.append(v1m[sub:] + v2[0])
    pad = jnp.full((sub - 2, v2m.shape[1]), -jnp.inf, F32)
    blocks.append(jnp.concatenate([v1[2 * sub] + v2[0], v1[0] + v2[2 * sub], pad], axis=0))
    return jnp.concatenate(blocks, axis=0)


def _peer_topk_kernel(qt_ref, k1_ref, k2_ref, s1_ref, s2_ref, st_ref, *, n_heads, n_keys):
    half = k1_ref.shape[1]
    hp = lax.Precision.HIGHEST
    k1 = k1_ref[...]
    k2 = k2_ref[...]
    taus = []
    for h in range(n_heads):
        base = h * 2 * half
        s1 = jnp.dot(k1, qt_ref[base:base + half, :], preferred_element_type=F32, precision=hp)
        s2 = jnp.dot(k2, qt_ref[base + half:base + 2 * half, :], preferred_element_type=F32,
                     precision=hp)
        a1 = s1 * LOG2E
        a2 = s2 * LOG2E
        v1 = _kth_largest_rows(a1, PEER_TOPK + 1)
        v2 = _kth_largest_rows(a2, PEER_TOPK + 1)
        v1s = [v - v1[0] for v in v1]
        v2s = [v - v2[0] for v in v2]
        cand = _staircase_candidates(v1s, v2s)
        picked = cand >= _kth_largest_value(cand, PEER_TOPK)
        zsum = jnp.sum(jnp.where(picked, jnp.exp2(cand), 0.0), axis=0, keepdims=True)
        shift1 = v1[0] + jnp.log2(zsum) + 1.0
        cand_z = _staircase_candidates([v - shift1 for v in v1], v2s)
        last_in = jnp.min(jnp.where(picked, cand_z, jnp.inf), axis=0, keepdims=True)
        first_out = jnp.max(jnp.where(picked, -jnp.inf, cand_z), axis=0, keepdims=True)
        taus.append(0.5 * (last_in + first_out))
        s1_ref[h * n_keys:(h + 1) * n_keys, :] = a1 - shift1
        s2_ref[h * n_keys:(h + 1) * n_keys, :] = a2 - v2[0]
    st_ref[...] = jnp.concatenate(taus, axis=0)


def _peer_topk(qt, k1, k2, tn=256):
    hq, t = qt.shape
    n_keys, half = k1.shape
    n_heads = hq // (2 * half)
    srows = n_heads * n_keys
    col = lambda rows: pl.BlockSpec((rows, tn), lambda i: (0, i))
    return pl.pallas_call(
        functools.partial(_peer_topk_kernel, n_heads=n_heads, n_keys=n_keys),
        grid=(t // tn,),
        in_specs=[col(hq),
                  pl.BlockSpec((n_keys, half), lambda i: (0, 0)),
                  pl.BlockSpec((n_keys, half), lambda i: (0, 0))],
        out_specs=[col(srows), col(srows), col(n_heads)],
        out_shape=[jax.ShapeDtypeStruct((srows, t), F32), jax.ShapeDtypeStruct((srows, t), F32),
                   jax.ShapeDtypeStruct((n_heads, t), F32)],
        compiler_params=_params(("parallel",)),
        name="peer_topk",
    )(qt, k1.astype(F32), k2.astype(F32))


def _peer_dense_kernel(h_ref, u_ref, v_ref, l1_ref, l2_ref, tau_ref, o_ref, tau8, l1b, wt, out_acc,
                       *, n_heads, n_keys):
    j = pl.program_id(1)
    te = v_ref.shape[0]
    tm = h_ref.shape[0]
    d = v_ref.shape[1]
    na = te // n_keys
    sub = 8
    dc = 1024

    @pl.when(j == 0)
    def _():
        out_acc[...] = jnp.zeros_like(out_acc)
        for h in range(n_heads):
            tau8[h * sub:(h + 1) * sub, :] = jnp.broadcast_to(tau_ref[h:h + 1, :], (sub, tm))

    for aa in range(na):
        for h in range(n_heads):
            row = l1_ref[pl.ds(h * n_keys + j * na + aa, 1), :]
            k = (aa * n_heads + h) * sub
            l1b[k:k + sub, :] = jnp.broadcast_to(row, (sub, tm))

    ax = 2
    link = jnp.zeros((sub, tm), F32)
    for a0 in range(0, na, ax):
        for b0 in range(0, n_keys, 2 * sub):
            acc = [[link, link] for _ in range(ax)]
            for h in range(n_heads):
                t8 = tau8[h * sub:(h + 1) * sub, :]
                r2 = [slice(h * n_keys + b0 + y * sub, h * n_keys + b0 + (y + 1) * sub) for y in range(2)]
                l2v = [l2_ref[r, :] for r in r2]
                for x in range(ax):
                    k = ((a0 + x) * n_heads + h) * sub
                    l1v = l1b[k:k + sub, :]
                    for y in range(2):
                        logit = l1v + l2v[y]
                        acc[x][y] = acc[x][y] + jnp.exp2(jnp.where(logit >= t8, logit, -jnp.inf))
            for x in range(ax):
                r0 = (a0 + x) * n_keys + b0
                wt[r0:r0 + 2 * sub, :] = jnp.concatenate(acc[x], axis=0).astype(wt.dtype)
            link = jnp.minimum(pltpu.roll(pltpu.roll(acc[0][0], 1, 1), 1, 1), 0.0)

    w = wt[...].T
    act = lax.dot_general(h_ref[...], u_ref[...], (((1,), (1,)), ((), ())),
                          preferred_element_type=F32)
    inner = act * (GELU_C0 + GELU_C1 * (act * act))
    g = (act * (1.0 + jnp.tanh(inner))).astype(BF16) * w
    for c0 in range(0, d, dc):
        out_acc[:, c0:c0 + dc] += jnp.dot(g, v_ref[:, c0:c0 + dc], preferred_element_type=F32)

    @pl.when(j == pl.num_programs(1) - 1)
    def _():
        o_ref[...] = out_acc[...].astype(o_ref.dtype)


def _peer_dense(h2, u_tab, v_tab, layer, s1t, s2t, stats, tm=512, te=512):
    t, d = h2.shape
    ne = v_tab.shape[1]
    n_keys = int(round(math.sqrt(ne)))
    n_heads = s1t.shape[0] // n_keys
    srows = s1t.shape[0]
    na = te // n_keys
    tok = lambda rows: pl.BlockSpec((rows, tm), lambda i, j: (0, i))
    return pl.pallas_call(
        functools.partial(_peer_dense_kernel, n_heads=n_heads, n_keys=n_keys),
        grid=(t // tm, ne // te),
        in_specs=[pl.BlockSpec((tm, d), lambda i, j: (i, 0)),
                  pl.BlockSpec((None, te, d), lambda i, j: (layer, j, 0)),
                  pl.BlockSpec((None, te, d), lambda i, j: (layer, j, 0)),
                  tok(srows), tok(srows), tok(n_heads)],
        out_specs=pl.BlockSpec((tm, d), lambda i, j: (i, 0)),
        out_shape=jax.ShapeDtypeStruct((t, d), BF16),
        scratch_shapes=[pltpu.VMEM((n_heads * 8, tm), F32),
                        pltpu.VMEM((na * n_heads * 8, tm), F32),
                        pltpu.VMEM((te, tm), BF16),
                        pltpu.VMEM((tm, d), F32)],
        compiler_params=_params(("parallel", "arbitrary")),
        name="peer_dense",
    )(h2, u_tab, v_tab, s1t, s2t, stats)


def kernel(x, c, w_ada, b_ada, ada_layer, norm1_g, norm2_g, w_in, lam_re, lam_im, log_dt, b_re, b_im,
           c_re, c_im, d_skip, w_glu, q_gain, k_gain, sinks, gn_ssm, gn_attn, w_out, peer_wq, peer_k1,
           peer_k2, peer_u, peer_v):
    bsz, seq, d = x.shape
    t = bsz * seq
    depth = w_in.shape[0]
    n_mod = ada_layer.shape[1]
    ssm_w = w_glu.shape[1]
    attn_w = gn_attn.shape[1]
    kv_w = (w_in.shape[2] - ssm_w - attn_w) // 2

    w_in_b, w_glu_b, w_out_b = w_in, w_glu.astype(BF16), w_out
    wq_t_b = peer_wq.transpose(0, 2, 1).astype(BF16)
    u_b, v_b = peer_u.astype(BF16), peer_v.astype(BF16)

    s5_ops = jax.vmap(_s5_operators)(lam_re, lam_im, log_dt, b_re, b_im, c_re, c_im, d_skip)

    cond = _cond(c, w_ada, b_ada).reshape(bsz, n_mod, d)
    xf = x.astype(F32).reshape(t, d)
    delta, gate_prev = None, None
    for l in range(depth):
        mod = cond + ada_layer[l].astype(F32)
        shift1, scale1, gate1, shift2, scale2, gate2 = (mod[:, i] for i in range(n_mod))

        xf, h = _adaln_norm(xf, delta, gate_prev, norm1_g[l], scale1, shift1, seq)
        z_ssm = _matmul_channel_blocks(h, w_in_b, l, ssm_w, BF16, tm=1024, tn=512, steps=S5_CHUNK)
        z_attn = _matmul(h, w_in_b, l, ssm_w, attn_w + 2 * kv_w, BF16, tm=1024, tn=512)
        g = _s5_core(z_ssm, s5_ops, l, bsz)
        y_ssm = _glu_norm(g, w_glu_b, l, gn_ssm[l], steps=S5_CHUNK)
        y_attn = _attention(z_attn, 0, attn_w, kv_w, q_gain[l], k_gain[l], sinks[l], gn_attn[l], bsz)
        mixed = _matmul_concat(y_ssm, y_attn, w_out_b, l, BF16, tm=1024, tn=512)

        xf, h2 = _adaln_norm(xf, mixed, gate1, norm2_g[l], scale2, shift2, seq)
        qt = _matmul_nt(wq_t_b, l, h2, F32, tn=512)
        s1t, s2t, stats = _peer_topk(qt, peer_k1[l], peer_k2[l])
        delta = _peer_dense(h2, u_b, v_b, l, s1t, s2t, stats)
        gate_prev = gate2
    out = _residual_add(xf, delta, gate_prev, seq)
    return out.reshape(bsz, seq, d).astype(x.dtype)
```

```python
import functools
import math

import jax
import jax.numpy as jnp
from jax import lax
from jax.experimental import pallas as pl
from jax.experimental.pallas import tpu as pltpu

F32 = jnp.float32
BF16 = jnp.bfloat16
EPS = 1e-6
WINDOW = 128
PEER_TOPK = 16
LOG2E = 1.4426950408889634
GELU_C0 = math.sqrt(2.0 / math.pi)
GELU_C1 = 0.044715 * GELU_C0
S5_CHUNK = 16
S5_GROUP_BLOCK = 8
V7X_VMEM_LIMIT = 56 * 1024 * 1024


def _params(semantics, vmem=V7X_VMEM_LIMIT, flags=None):
    return pltpu.CompilerParams(dimension_semantics=semantics, vmem_limit_bytes=vmem, flags=flags)


def _mm_kernel(a_ref, b_ref, o_ref):
    o_ref[...] = jnp.dot(a_ref[...], b_ref[...].astype(a_ref.dtype),
                         preferred_element_type=F32).astype(o_ref.dtype)


def _matmul(a, b, layer, col0, n, out_dtype, tm, tn):
    m, k = a.shape
    j0 = col0 // tn
    return pl.pallas_call(
        _mm_kernel,
        grid=(m // tm, n // tn),
        in_specs=[pl.BlockSpec((tm, k), lambda i, j: (i, 0)),
                  pl.BlockSpec((None, k, tn), lambda i, j: (layer, 0, j0 + j))],
        out_specs=pl.BlockSpec((tm, tn), lambda i, j: (i, j)),
        out_shape=jax.ShapeDtypeStruct((m, n), out_dtype),
        compiler_params=_params(("parallel", "parallel")),
        name="matmul",
    )(a, b)


def _mm_cb_kernel(a_ref, b_ref, o_ref, scr):
    res = jnp.dot(a_ref[...], b_ref[...].astype(a_ref.dtype), preferred_element_type=F32)
    nb, rows, width = o_ref.shape
    lanes = scr.shape[2]
    steps = width // lanes
    for k in range(nb):
        scr[k] = res[:, k * lanes:(k + 1) * lanes]
    for k in range(nb):
        for s in range(steps):
            o_ref[k, :, s * lanes:(s + 1) * lanes] = (
                scr[k, pl.ds(s, rows, stride=steps), :].astype(o_ref.dtype))


def _matmul_channel_blocks(a, b, layer, n, out_dtype, tm, tn, steps, lanes=128):
    m, k = a.shape
    return pl.pallas_call(
        _mm_cb_kernel,
        grid=(m // tm, n // tn),
        in_specs=[pl.BlockSpec((tm, k), lambda i, j: (i, 0)),
                  pl.BlockSpec((None, k, tn), lambda i, j: (layer, 0, j))],
        out_specs=pl.BlockSpec((tn // lanes, tm // steps, steps * lanes), lambda i, j: (j, i, 0)),
        out_shape=jax.ShapeDtypeStruct((n // lanes, m // steps, steps * lanes), out_dtype),
        scratch_shapes=[pltpu.VMEM((tn // lanes, tm, lanes), F32)],
        compiler_params=_params(("parallel", "parallel")),
        name="matmul_channel_blocks",
    )(a, b)


def _mm2_kernel(a1_ref, a2_ref, b1_ref, b2_ref, o_ref):
    acc = jnp.dot(a1_ref[...], b1_ref[...].astype(a1_ref.dtype), preferred_element_type=F32)
    acc += jnp.dot(a2_ref[...], b2_ref[...].astype(a2_ref.dtype), preferred_element_type=F32)
    o_ref[...] = acc.astype(o_ref.dtype)


def _matmul_concat(a1, a2, b, layer, out_dtype, tm, tn):
    m, k1 = a1.shape
    k2 = a2.shape[1]
    assert k1 == k2
    n = b.shape[2]
    return pl.pallas_call(
        _mm2_kernel,
        grid=(m // tm, n // tn),
        in_specs=[pl.BlockSpec((tm, k1), lambda i, j: (i, 0)),
                  pl.BlockSpec((tm, k2), lambda i, j: (i, 0)),
                  pl.BlockSpec((None, k1, tn), lambda i, j: (layer, 0, j)),
                  pl.BlockSpec((None, k2, tn), lambda i, j: (layer, 1, j))],
        out_specs=pl.BlockSpec((tm, tn), lambda i, j: (i, j)),
        out_shape=jax.ShapeDtypeStruct((m, n), out_dtype),
        compiler_params=_params(("parallel", "parallel")),
        name="matmul_concat",
    )(a1, a2, b, b)


def _mm_nt_kernel(a_ref, b_ref, o_ref):
    o_ref[...] = lax.dot_general(a_ref[...], b_ref[...], (((1,), (1,)), ((), ())),
                                 preferred_element_type=F32).astype(o_ref.dtype)


def _matmul_nt(a, layer, b, out_dtype, tn):
    _, m, k = a.shape
    n = b.shape[0]
    return pl.pallas_call(
        _mm_nt_kernel,
        grid=(n // tn,),
        in_specs=[pl.BlockSpec((None, m, k), lambda j: (layer, 0, 0)),
                  pl.BlockSpec((tn, k), lambda j: (j, 0))],
        out_specs=pl.BlockSpec((m, tn), lambda j: (0, j)),
        out_shape=jax.ShapeDtypeStruct((m, n), out_dtype),
        compiler_params=_params(("parallel",)),
        name="matmul_nt",
    )(a, b)


def _cond_kernel(c_ref, w_ref, b_ref, o_ref):
    c = c_ref[...]
    s = c * jax.nn.sigmoid(c)
    o_ref[...] = jnp.dot(s, w_ref[...], preferred_element_type=F32,
                         precision=lax.Precision.HIGHEST) + b_ref[...]


def _cond(c, w_ada, b_ada, tn=512):
    bsz, d = c.shape
    n = w_ada.shape[1]
    rows = 8
    cp = jnp.zeros((rows, d), F32).at[:bsz].set(c.astype(F32))
    out = pl.pallas_call(
        _cond_kernel,
        grid=(n // tn,),
        in_specs=[pl.BlockSpec((rows, d), lambda j: (0, 0)),
                  pl.BlockSpec((d, tn), lambda j: (0, j)),
                  pl.BlockSpec((1, tn), lambda j: (0, j))],
        out_specs=pl.BlockSpec((rows, tn), lambda j: (0, j)),
        out_shape=jax.ShapeDtypeStruct((rows, n), F32),
        compiler_params=_params(("parallel",)),
        name="adaln_cond",
    )(cp, w_ada.astype(F32), b_ada.astype(F32).reshape(1, n))
    return out[:bsz]


def _norm_body(x, g_ref, scale_ref, shift_ref, h_ref):
    y = x * lax.rsqrt(jnp.mean(x * x, axis=-1, keepdims=True) + EPS)
    y = y * g_ref[...]
    h_ref[...] = (y * (1.0 + scale_ref[0]) + shift_ref[0]).astype(h_ref.dtype)


def _norm_kernel(x_ref, g_ref, scale_ref, shift_ref, h_ref):
    _norm_body(x_ref[...], g_ref, scale_ref, shift_ref, h_ref)


def _resnorm_kernel(x_ref, d_ref, gate_ref, g_ref, scale_ref, shift_ref, xo_ref, h_ref):
    x = x_ref[...] + gate_ref[0] * d_ref[...]
    xo_ref[...] = x
    _norm_body(x, g_ref, scale_ref, shift_ref, h_ref)


def _adaln_norm(x, delta, gate, g, scale, shift, rows_per_batch, tm=256):
    t, d = x.shape
    tpb = rows_per_batch // tm
    row = pl.BlockSpec((tm, d), lambda i: (i, 0))
    per_batch = pl.BlockSpec((1, 1, d), lambda i: (i // tpb, 0, 0))
    vec = pl.BlockSpec((1, d), lambda i: (0, 0))
    g2 = g.astype(F32).reshape(1, d)
    b3 = lambda v: v.astype(F32).reshape(v.shape[0], 1, d)
    if delta is None:
        h = pl.pallas_call(
            _norm_kernel,
            grid=(t // tm,),
            in_specs=[row, vec, per_batch, per_batch],
            out_specs=row,
            out_shape=jax.ShapeDtypeStruct((t, d), BF16),
            compiler_params=_params(("parallel",)),
            name="adaln_norm",
        )(x, g2, b3(scale), b3(shift))
        return x, h
    return pl.pallas_call(
        _resnorm_kernel,
        grid=(t // tm,),
        in_specs=[row, row, per_batch, vec, per_batch, per_batch],
        out_specs=[row, row],
        out_shape=[jax.ShapeDtypeStruct((t, d), F32), jax.ShapeDtypeStruct((t, d), BF16)],
        compiler_params=_params(("parallel",)),
        name="residual_adaln_norm",
    )(x, delta, b3(gate), g2, b3(scale), b3(shift))


def _residual_kernel(x_ref, d_ref, gate_ref, o_ref):
    o_ref[...] = x_ref[...] + gate_ref[0] * d_ref[...]


def _residual_add(x, delta, gate, rows_per_batch, tm=256):
    t, d = x.shape
    tpb = rows_per_batch // tm
    row = pl.BlockSpec((tm, d), lambda i: (i, 0))
    return pl.pallas_call(
        _residual_kernel,
        grid=(t // tm,),
        in_specs=[row, row, pl.BlockSpec((1, 1, d), lambda i: (i // tpb, 0, 0))],
        out_specs=row,
        out_shape=jax.ShapeDtypeStruct((t, d), F32),
        compiler_params=_params(("parallel",)),
        name="residual_add",
    )(x, delta, gate.astype(F32).reshape(gate.shape[0], 1, d))


def _s5_operators(lam_re, lam_im, log_dt, b_re, b_im, c_re, c_im, d_skip):
    hp = lax.Precision.HIGHEST
    j = S5_CHUNK
    f = lambda v: v.astype(F32)
    lr, li = f(lam_re), f(lam_im)
    g, p = lr.shape
    h = b_re.shape[-1]
    dt = jnp.exp(f(log_dt))[:, None]
    steps = jnp.arange(j + 1, dtype=F32)[:, None, None]
    pmag = jnp.exp(steps * (lr * dt))
    pr = pmag * jnp.cos(steps * (li * dt))
    pi = pmag * jnp.sin(steps * (li * dt))
    ar, ai = pr[1], pi[1]
    den = lr * lr + li * li
    kr = ((ar - 1.0) * lr + ai * li) / den
    ki = (ai * lr - (ar - 1.0) * li) / den
    br, bi = f(b_re), f(b_im)
    bbr = kr[..., None] * br - ki[..., None] * bi
    bbi = kr[..., None] * bi + ki[..., None] * br
    cr, ci = f(c_re), f(c_im)
    er = pr[:j, :, :, None] * bbr - pi[:j, :, :, None] * bbi
    ei = pr[:j, :, :, None] * bbi + pi[:j, :, :, None] * bbr
    kk = (jnp.einsum('ghp,ngpk->nghk', cr, er, precision=hp)
          - jnp.einsum('ghp,ngpk->nghk', ci, ei, precision=hp))
    kk = kk.at[0].add(f(d_skip)[:, :, None] * jnp.eye(h, dtype=F32))
    gb = S5_GROUP_BLOCK
    nblk = g // gb
    t_op = kk.reshape(j, nblk, gb, h, h).transpose(1, 0, 4, 2, 3).reshape(nblk, j, h, gb * h)

    prr = pr[j - 1::-1].reshape(j, nblk, gb, p).transpose(1, 0, 2, 3)[:, :, None]
    pir = pi[j - 1::-1].reshape(j, nblk, gb, p).transpose(1, 0, 2, 3)[:, :, None]
    bbr_e = bbr.reshape(nblk, gb, p, h).transpose(0, 3, 1, 2)[:, None]
    bbi_e = bbi.reshape(nblk, gb, p, h).transpose(0, 3, 1, 2)[:, None]
    bc_r = (prr * bbr_e - pir * bbi_e).reshape(nblk, j * h, gb * p)
    bc_i = (prr * bbi_e + pir * bbr_e).reshape(nblk, j * h, gb * p)

    cr_e = cr.reshape(nblk, gb, h, p).transpose(0, 3, 1, 2)[:, :, None]
    ci_e = ci.reshape(nblk, gb, h, p).transpose(0, 3, 1, 2)[:, :, None]
    p1r = pr[1:].reshape(j, nblk, gb, p).transpose(1, 3, 0, 2)[..., None]
    p1i = pi[1:].reshape(j, nblk, gb, p).transpose(1, 3, 0, 2)[..., None]
    cc_r = (cr_e * p1r - ci_e * p1i).reshape(nblk, p, j * gb * h)
    cc_i = (-(cr_e * p1i + ci_e * p1r)).reshape(nblk, p, j * gb * h)

    a16r = pr[j].reshape(nblk, 1, gb * p)
    a16i = pi[j].reshape(nblk, 1, gb * p)
    return (t_op.astype(BF16), bc_r.astype(BF16), bc_i.astype(BF16),
            cc_r.astype(BF16), cc_i.astype(BF16), a16r, a16i)


def _s5_kernel(u_ref, t_ref, bre_ref, bim_ref, cre_ref, cim_ref, ar_ref, ai_ref, mt_ref, mb_ref,
               o_ref, tblk, bblk_r, bblk_i, cblk_r, cblk_i, zre, zim, pre, pim,
               *, chunks_per_batch, n_batch):
    gb = S5_GROUP_BLOCK
    j = S5_CHUNK
    hh = t_ref.shape[2]
    pp = cre_ref.shape[1]
    lanes = gb * hh
    ncol = 4 * lanes
    for i in range(j):
        j_first = (i * lanes // ncol) * ncol // lanes
        for gl in range(gb):
            src = slice(i * hh, (i + 1) * hh)
            dst = slice(i * lanes + gl * hh, i * lanes + (gl + 1) * hh)
            m_gl = mt_ref[gl, :, 0:lanes]
            for jj in range(j_first, j):
                blk = t_ref[0, jj - i] * m_gl if jj >= i else jnp.zeros((hh, lanes), tblk.dtype)
                tblk[dst, jj * lanes:(jj + 1) * lanes] = blk
            bblk_r[dst, :] = bre_ref[0, src, :] * mb_ref[gl]
            bblk_i[dst, :] = bim_ref[0, src, :] * mb_ref[gl]
    for gl in range(gb):
        for r0 in range(0, pp, hh):
            src = slice(r0, r0 + hh)
            dst = slice(gl * pp + r0, gl * pp + r0 + hh)
            cblk_r[dst, :] = cre_ref[0, src, :] * mt_ref[gl]
            cblk_i[dst, :] = cim_ref[0, src, :] * mt_ref[gl]

    u = u_ref[0]
    zre[...] = jnp.dot(u, bblk_r[...], preferred_element_type=F32)
    zim[...] = jnp.dot(u, bblk_i[...], preferred_element_type=F32)
    ar = ar_ref[0]
    ai = ai_ref[0]

    def step(c, carry):
        new = []
        for b in range(n_batch):
            sr, si = carry[2 * b], carry[2 * b + 1]
            row = pl.ds(b * chunks_per_batch + c, 1)
            pre[row, :] = sr
            pim[row, :] = si
            new.append(ar * sr - ai * si + zre[row, :])
            new.append(ar * si + ai * sr + zim[row, :])
        return tuple(new)

    zero = jnp.zeros((1, zre.shape[1]), F32)
    lax.fori_loop(0, chunks_per_batch, step, (zero,) * (2 * n_batch), unroll=8)

    pr_b = pre[...].astype(BF16)
    pi_b = pim[...].astype(BF16)
    for c0 in range(0, j * lanes, ncol):
        cols = slice(c0, c0 + ncol)
        k_rows = c0 + ncol
        y = jnp.dot(u[:, :k_rows], tblk[:k_rows, cols], preferred_element_type=F32)
        y += jnp.dot(pr_b, cblk_r[:, cols], preferred_element_type=F32)
        y += jnp.dot(pi_b, cblk_i[:, cols], preferred_element_type=F32)
        o_ref[0, :, cols] = jax.nn.gelu(y).astype(o_ref.dtype)


def _s5_core(uc, ops, layer, n_batch):
    t_op, bre, bim, cre, cim, a16r, a16i = ops
    j = S5_CHUNK
    gb = S5_GROUP_BLOCK
    nblk, nchunk, width = uc.shape
    lanes = width // j
    hh = lanes // gb
    pp = cre.shape[2]
    def own(width, per_group):
        col_group = jnp.arange(width) // per_group % gb
        m = col_group[None, None, :] == jnp.arange(gb)[:, None, None]
        return jnp.broadcast_to(m, (gb, hh, width)).astype(BF16)

    mask_t = own(j * lanes, hh)
    mask_b = own(gb * pp, pp)
    blk = lambda r, c: pl.BlockSpec((1, r, c), lambda i: (i, 0, 0))
    lblk = lambda r, c: pl.BlockSpec((None, 1, r, c), lambda i: (layer, i, 0, 0))
    const = lambda a: pl.BlockSpec(a.shape, lambda i: (0, 0, 0))
    yc = pl.pallas_call(
        functools.partial(_s5_kernel, chunks_per_batch=nchunk // n_batch, n_batch=n_batch),
        grid=(nblk,),
        in_specs=[blk(nchunk, j * lanes),
                  pl.BlockSpec((None, 1, j, hh, lanes), lambda i: (layer, i, 0, 0, 0)),
                  lblk(j * hh, gb * pp), lblk(j * hh, gb * pp),
                  lblk(pp, j * lanes), lblk(pp, j * lanes),
                  lblk(1, gb * pp), lblk(1, gb * pp), const(mask_t), const(mask_b)],
        out_specs=blk(nchunk, j * lanes),
        out_shape=jax.ShapeDtypeStruct((nblk, nchunk, j * lanes), BF16),
        scratch_shapes=[pltpu.VMEM((j * lanes, j * lanes), BF16),
                        pltpu.VMEM((j * lanes, gb * pp), BF16), pltpu.VMEM((j * lanes, gb * pp), BF16),
                        pltpu.VMEM((gb * pp, j * lanes), BF16), pltpu.VMEM((gb * pp, j * lanes), BF16)]
                       + [pltpu.VMEM((nchunk, gb * pp), F32)] * 4,
        compiler_params=_params(("parallel",)),
        name="s5_chunked_scan",
    )(uc, t_op, bre, bim, cre, cim, a16r, a16i, mask_t, mask_b)
    return yc


def _glu_norm_kernel(g_ref, w_ref, gn_ref, o_ref, scr):
    nblk, rows, width = g_ref.shape
    lanes = scr.shape[2]
    steps = width // lanes
    for cb in range(nblk):
        for s in range(steps):
            scr[cb, pl.ds(s, rows, stride=steps), :] = (
                g_ref[cb, :, s * lanes:(s + 1) * lanes].astype(F32))
    g32 = jnp.concatenate([scr[cb] for cb in range(nblk)], axis=1)
    gate = jax.nn.sigmoid(jnp.dot(g32.astype(BF16), w_ref[...], preferred_element_type=F32))
    y = g32 * gate
    y = y * lax.rsqrt(jnp.mean(y * y, axis=-1, keepdims=True) + EPS)
    o_ref[...] = (y * gn_ref[...]).astype(o_ref.dtype)


def _glu_norm(gc, w_glu, layer, gn, steps, tm=512, lanes=128):
    nblk, nrow, width = gc.shape
    t = nrow * steps
    w = nblk * lanes
    return pl.pallas_call(
        _glu_norm_kernel,
        grid=(t // tm,),
        in_specs=[pl.BlockSpec((nblk, tm // steps, width), lambda i: (0, i, 0)),
                  pl.BlockSpec((None, w, w), lambda i: (layer, 0, 0)),
                  pl.BlockSpec((1, w), lambda i: (0, 0))],
        out_specs=pl.BlockSpec((tm, w), lambda i: (i, 0)),
        out_shape=jax.ShapeDtypeStruct((t, w), BF16),
        scratch_shapes=[pltpu.VMEM((nblk, tm, lanes), F32)],
        compiler_params=_params(("parallel",)),
        name="glu_groupnorm",
    )(gc, w_glu, gn.astype(F32).reshape(1, w))


def _head_rms_scale(x, sel_ref, exp_ref, head_dim):
    ss = jnp.dot((x * x).astype(BF16), sel_ref[...], preferred_element_type=F32)
    r = lax.rsqrt(ss * (1.0 / head_dim) + EPS)
    r_hi = r.astype(BF16)
    r_lo = (r - r_hi.astype(F32)).astype(BF16)
    return (jnp.dot(r_hi, exp_ref[...], preferred_element_type=F32)
            + jnp.dot(r_lo, exp_ref[...], preferred_element_type=F32))


def _attn_kernel(sink_ref, q_ref, kp_ref, kc_ref, vp_ref, vc_ref, qg_ref, kg_ref, gn_ref,
                 qsel_ref, qexp_ref, ksel_ref, kexp_ref, o_ref, acc,
                 *, n_q_heads, n_kv_heads, head_dim):
    nblk = pl.program_id(1)
    w = WINDOW
    lw = 2 * head_dim
    grp = n_q_heads // n_kv_heads
    t_loc = lax.broadcasted_iota(jnp.int32, (w, 2 * w), 0)
    s_loc = lax.broadcasted_iota(jnp.int32, (w, 2 * w), 1)
    dist = t_loc + w - s_loc
    valid = (dist >= 0) & (dist < w) & (s_loc + nblk * w >= w)
    dmask = jnp.where(valid, dist.astype(F32), jnp.inf)

    q = q_ref[...].astype(F32)
    qn = (q * _head_rms_scale(q, qsel_ref, qexp_ref, head_dim) * qg_ref[...]).astype(BF16)
    k = jnp.concatenate([kp_ref[...], kc_ref[...]], axis=0).astype(F32)
    kn = k * _head_rms_scale(k, ksel_ref, kexp_ref, head_dim) * kg_ref[...]
    v = jnp.concatenate([vp_ref[...], vc_ref[...]], axis=0).astype(F32)

    lane = lax.broadcasted_iota(jnp.int32, (2 * w, lw), 1)
    lo = lane < head_dim

    def block_diag(x, hk):
        grp_lanes = x[:, (hk // 2) * lw:(hk // 2 + 1) * lw]
        swapped = pltpu.roll(grp_lanes, head_dim, 1)
        low, high = (grp_lanes, swapped) if hk % 2 == 0 else (swapped, grp_lanes)
        return jnp.concatenate([jnp.where(lo, low, 0.0), jnp.where(lo, 0.0, high)],
                               axis=0).astype(BF16)

    lane_q = lax.broadcasted_iota(jnp.int32, (w, lw), 1) < head_dim
    for hk in range(n_kv_heads):
        kb = block_diag(kn, hk)
        vb = block_diag(v, hk)
        for pq in range(grp // 2):
            pair = hk * (grp // 2) + pq
            cols = slice(pair * lw, (pair + 1) * lw)
            s = lax.dot_general(qn[:, cols], kb, (((1,), (1,)), ((), ())),
                                preferred_element_type=F32)
            es, inv = [], []
            for i in range(2):
                hq = 2 * pair + i
                slope = 2.0 ** (-8.0 * (hq + 1) / n_q_heads)
                si = s[:, i * 2 * w:(i + 1) * 2 * w] - slope * dmask
                sink = sink_ref[hq]
                m = jnp.maximum(jnp.max(si, axis=-1, keepdims=True), sink)
                e = jnp.exp(si - m)
                inv.append(1.0 / (jnp.sum(e, axis=-1, keepdims=True) + jnp.exp(sink - m)))
                es.append(e.astype(BF16))
            pv = jnp.dot(jnp.concatenate(es, axis=1), vb, preferred_element_type=F32)
            acc[:, cols] = pv * jnp.where(lane_q, inv[0], inv[1])
    y = acc[...]
    y = y * lax.rsqrt(jnp.mean(y * y, axis=-1, keepdims=True) + EPS)
    o_ref[...] = (y * gn_ref[...]).astype(o_ref.dtype)


def _attention(z, ssm_w, attn_w, kv_w, q_gain, k_gain, sinks, gn, n_batch):
    t = z.shape[0]
    w = WINDOW
    head_dim = q_gain.shape[0]
    n_q = attn_w // head_dim
    n_kv = kv_w // head_dim
    nb = t // n_batch // w
    qcol = ssm_w // attn_w
    kcol = (ssm_w + attn_w) // kv_w
    vcol = kcol + 1
    cur = lambda col: (lambda b, n: (b * nb + n, col))
    prev = lambda col: (lambda b, n: (b * nb + jnp.maximum(n - 1, 0), col))
    full = lambda r, c: pl.BlockSpec((r, c), lambda b, n: (0, 0))
    nsel = 128

    def selectors(width):
        sel = (jnp.arange(width)[:, None] // head_dim == jnp.arange(nsel)[None, :]).astype(BF16)
        return sel, sel.T

    qsel, qexp = selectors(attn_w)
    ksel, kexp = selectors(kv_w)
    qg_row = (jnp.tile(q_gain.astype(F32), n_q) * head_dim ** -0.5).reshape(1, attn_w)
    kg_row = jnp.tile(k_gain.astype(F32), n_kv).reshape(1, kv_w)
    return pl.pallas_call(
        functools.partial(_attn_kernel, n_q_heads=n_q, n_kv_heads=n_kv, head_dim=head_dim),
        grid=(n_batch, nb),
        in_specs=[pl.BlockSpec(memory_space=pltpu.SMEM),
                  pl.BlockSpec((w, attn_w), cur(qcol)),
                  pl.BlockSpec((w, kv_w), prev(kcol)),
                  pl.BlockSpec((w, kv_w), cur(kcol)),
                  pl.BlockSpec((w, kv_w), prev(vcol)),
                  pl.BlockSpec((w, kv_w), cur(vcol)),
                  full(1, attn_w), full(1, kv_w), full(1, attn_w),
                  full(attn_w, nsel), full(nsel, attn_w), full(kv_w, nsel), full(nsel, kv_w)],
        out_specs=pl.BlockSpec((w, attn_w), lambda b, n: (b * nb + n, 0)),
        out_shape=jax.ShapeDtypeStruct((t, attn_w), BF16),
        scratch_shapes=[pltpu.VMEM((w, attn_w), F32)],
        compiler_params=_params(("parallel", "parallel")),
        name="swa_attention",
    )(sinks.astype(F32), z, z, z, z, z, qg_row, kg_row, gn.astype(F32).reshape(1, attn_w),
      qsel, qexp, ksel, kexp)


def _sorting_network(n):
    pairs = []
    p = 1
    while p < n:
        k = p
        while k >= 1:
            for j in range(k % p, n - k, 2 * k):
                for i in range(min(k, n - j - k)):
                    if (i + j) // (2 * p) == (i + j + k) // (2 * p):
                        pairs.append((i + j, i + j + k))
            k //= 2
        p *= 2
    return pairs


def _kth_largest_rows(vals, k):
    sub = 8
    tiles = [vals[r:r + sub, :] for r in range(0, vals.shape[0], sub)]
    for i, j in _sorting_network(len(tiles)):
        hi = jnp.maximum(tiles[i], tiles[j])
        tiles[j] = jnp.minimum(tiles[i], tiles[j])
        tiles[i] = hi
    out = []
    for it in range(k):
        m = jnp.max(tiles[0], axis=0, keepdims=True)
        out.append(m)
        hit = tiles[0] == m
        depth = min(len(tiles), k - it)
        for r in range(depth - 1):
            tiles[r] = jnp.where(hit, tiles[r + 1], tiles[r])
        if depth == len(tiles):
            tiles[depth - 1] = jnp.where(hit, -jnp.inf, tiles[depth - 1])
    return out


def _kth_largest_staircase(cand, k):
    sub = 8
    lists = [cand[0:sub]] + [cand[(2 + i) * sub:(3 + i) * sub] for i in range(sub - 1)]
    singles = [cand[sub:2 * sub], cand[9 * sub:10 * sub], cand[10 * sub:11 * sub]]
    left = jnp.full((1, cand.shape[1]), float(k), F32)
    kth = jnp.full((1, cand.shape[1]), -jnp.inf, F32)
    for it in range(k):
        heads = [lists[0]] + singles
        m = jnp.max(functools.reduce(jnp.maximum, heads), axis=0, keepdims=True)
        hits = [hd == m for hd in heads]
        count = functools.reduce(lambda a, b: a + b, [jnp.where(ht, 1.0, 0.0) for ht in hits])
        kth = jnp.where(left > 0.0, m, kth)
        left = left - jnp.sum(count, axis=0, keepdims=True)
        depth = min(len(lists), k - it)
        for r in range(depth - 1):
            lists[r] = jnp.where(hits[0], lists[r + 1], lists[r])
        if depth == len(lists):
            lists[depth - 1] = jnp.where(hits[0], -jnp.inf, lists[depth - 1])
        singles = [jnp.where(ht, -jnp.inf, sg) for ht, sg in zip(hits[1:], singles)]
    return kth


def _staircase_candidates(v1, v2):
    k = PEER_TOPK + 1
    sub = 8
    v1m = jnp.concatenate(v1[:2 * sub], axis=0)
    v2m = jnp.concatenate(v2[:2 * sub], axis=0)
    row = lax.broadcasted_iota(jnp.int32, (sub, v2m.shape[1]), 0)
    blocks = [v1[0] + v2m]
    for i in range(1, sub):
        keep = k // (i + 1)
        blk = v1[i] + v2m[0:sub]
        blocks.append(blk if keep >= sub else jnp.where(row < keep, blk, -jnp.inf))
    blocks.append(v1m[sub:] + v2[0])
    pad = jnp.full((sub - 2, v2m.shape[1]), -jnp.inf, F32)
    blocks.append(jnp.concatenate([v1[2 * sub] + v2[0], v1[0] + v2[2 * sub], pad], axis=0))
    return jnp.concatenate(blocks, axis=0)


def _peer_topk_kernel(qt_ref, k1_ref, k2_ref, s1_ref, s2_ref, st_ref, *, n_heads, n_keys):
    half = k1_ref.shape[1]
    hp = lax.Precision.HIGHEST
    k1 = k1_ref[...]
    k2 = k2_ref[...]
    taus = []
    for h in range(n_heads):
        base = h * 2 * half
        s1 = jnp.dot(k1, qt_ref[base:base + half, :], preferred_element_type=F32, precision=hp)
        s2 = jnp.dot(k2, qt_ref[base + half:base + 2 * half, :], preferred_element_type=F32,
                     precision=hp)
        a1 = s1 * LOG2E
        a2 = s2 * LOG2E
        v1 = _kth_largest_rows(a1, PEER_TOPK + 1)
        v2 = _kth_largest_rows(a2, PEER_TOPK + 1)
        v1s = [v - v1[0] for v in v1]
        v2s = [v - v2[0] for v in v2]
        cand = _staircase_candidates(v1s, v2s)
        picked = cand >= _kth_largest_staircase(cand, PEER_TOPK)
        zsum = jnp.sum(jnp.where(picked, jnp.exp2(cand), 0.0), axis=0, keepdims=True)
        shift1 = v1[0] + jnp.log2(zsum) + 1.0
        cand_z = _staircase_candidates([v - shift1 for v in v1], v2s)
        last_in = jnp.min(jnp.where(picked, cand_z, jnp.inf), axis=0, keepdims=True)
        first_out = jnp.max(jnp.where(picked, -jnp.inf, cand_z), axis=0, keepdims=True)
        taus.append(0.5 * (last_in + first_out))
        s1_ref[h * n_keys:(h + 1) * n_keys, :] = a1 - shift1
        s2_ref[h * n_keys:(h + 1) * n_keys, :] = a2 - v2[0]
    st_ref[...] = jnp.concatenate(taus, axis=0)


def _peer_topk(qt, k1, k2, tn=256):
    hq, t = qt.shape
    n_keys, half = k1.shape
    n_heads = hq // (2 * half)
    srows = n_heads * n_keys
    col = lambda rows: pl.BlockSpec((rows, tn), lambda i: (0, i))
    return pl.pallas_call(
        functools.partial(_peer_topk_kernel, n_heads=n_heads, n_keys=n_keys),
        grid=(t // tn,),
        in_specs=[col(hq),
                  pl.BlockSpec((n_keys, half), lambda i: (0, 0)),
                  pl.BlockSpec((n_keys, half), lambda i: (0, 0))],
        out_specs=[col(srows), col(srows), col(n_heads)],
        out_shape=[jax.ShapeDtypeStruct((srows, t), F32), jax.ShapeDtypeStruct((srows, t), F32),
                   jax.ShapeDtypeStruct((n_heads, t), F32)],
        compiler_params=_params(("parallel",)),
        name="peer_topk",
    )(qt, k1.astype(F32), k2.astype(F32))


def _peer_dense_kernel(h_ref, u_ref, v_ref, l1_ref, l2_ref, tau_ref, o_ref, tau8, l1b, wt, out_acc,
                       *, n_heads, n_keys):
    j = pl.program_id(1)
    te = v_ref.shape[0]
    tm = h_ref.shape[0]
    d = v_ref.shape[1]
    na = te // n_keys
    sub = 8
    dc = 1024

    @pl.when(j == 0)
    def _():
        out_acc[...] = jnp.zeros_like(out_acc)
        for h in range(n_heads):
            tau8[h * sub:(h + 1) * sub, :] = jnp.broadcast_to(tau_ref[h:h + 1, :], (sub, tm))

    for aa in range(na):
        for h in range(n_heads):
            row = l1_ref[pl.ds(h * n_keys + j * na + aa, 1), :]
            k = (aa * n_heads + h) * sub
            l1b[k:k + sub, :] = jnp.broadcast_to(row, (sub, tm))

    ax = 2
    link = jnp.zeros((sub, tm), F32)
    for a0 in range(0, na, ax):
        for b0 in range(0, n_keys, 2 * sub):
            acc = [[link, link] for _ in range(ax)]
            for h in range(n_heads):
                t8 = tau8[h * sub:(h + 1) * sub, :]
                r2 = [slice(h * n_keys + b0 + y * sub, h * n_keys + b0 + (y + 1) * sub) for y in range(2)]
                l2v = [l2_ref[r, :] for r in r2]
                for x in range(ax):
                    k = ((a0 + x) * n_heads + h) * sub
                    l1v = l1b[k:k + sub, :]
                    for y in range(2):
                        logit = l1v + l2v[y]
                        acc[x][y] = acc[x][y] + jnp.exp2(jnp.where(logit >= t8, logit, -jnp.inf))
            for x in range(ax):
                r0 = (a0 + x) * n_keys + b0
                wt[r0:r0 + 2 * sub, :] = jnp.concatenate(acc[x], axis=0).astype(wt.dtype)
            link = jnp.minimum(pltpu.roll(pltpu.roll(acc[0][0], 1, 1), 1, 1), 0.0)

    w = wt[...].T
    act = lax.dot_general(h_ref[...], u_ref[...], (((1,), (1,)), ((), ())),
                          preferred_element_type=F32)
    inner = act * (GELU_C0 + GELU_C1 * (act * act))
    g = (act * (1.0 + jnp.tanh(inner))).astype(BF16) * w
    for c0 in range(0, d, dc):
        out_acc[:, c0:c0 + dc] += jnp.dot(g, v_ref[:, c0:c0 + dc], preferred_element_type=F32)

    @pl.when(j == pl.num_programs(1) - 1)
    def _():
        o_ref[...] = out_acc[...].astype(o_ref.dtype)


def _peer_dense(h2, u_tab, v_tab, layer, s1t, s2t, stats, tm=512, te=512):
    t, d = h2.shape
    ne = v_tab.shape[1]
    n_keys = int(round(math.sqrt(ne)))
    n_heads = s1t.shape[0] // n_keys
    srows = s1t.shape[0]
    na = te // n_keys
    tok = lambda rows: pl.BlockSpec((rows, tm), lambda i, j: (0, i))
    return pl.pallas_call(
        functools.partial(_peer_dense_kernel, n_heads=n_heads, n_keys=n_keys),
        grid=(t // tm, ne // te),
        in_specs=[pl.BlockSpec((tm, d), lambda i, j: (i, 0)),
                  pl.BlockSpec((None, te, d), lambda i, j: (layer, j, 0)),
                  pl.BlockSpec((None, te, d), lambda i, j: (layer, j, 0)),
                  tok(srows), tok(srows), tok(n_heads)],
        out_specs=pl.BlockSpec((tm, d), lambda i, j: (i, 0)),
        out_shape=jax.ShapeDtypeStruct((t, d), BF16),
        scratch_shapes=[pltpu.VMEM((n_heads * 8, tm), F32),
                        pltpu.VMEM((na * n_heads * 8, tm), F32),
                        pltpu.VMEM((te, tm), BF16),
                        pltpu.VMEM((tm, d), F32)],
        compiler_params=_params(("parallel", "arbitrary")),
        name="peer_dense",
    )(h2, u_tab, v_tab, s1t, s2t, stats)


def kernel(x, c, w_ada, b_ada, ada_layer, norm1_g, norm2_g, w_in, lam_re, lam_im, log_dt, b_re, b_im,
           c_re, c_im, d_skip, w_glu, q_gain, k_gain, sinks, gn_ssm, gn_attn, w_out, peer_wq, peer_k1,
           peer_k2, peer_u, peer_v):
    bsz, seq, d = x.shape
    t = bsz * seq
    depth = w_in.shape[0]
    n_mod = ada_layer.shape[1]
    ssm_w = w_glu.shape[1]
    attn_w = gn_attn.shape[1]
    kv_w = (w_in.shape[2] - ssm_w - attn_w) // 2

    w_in_b, w_glu_b, w_out_b = w_in, w_glu.astype(BF16), w_out
    wq_t_b = peer_wq.transpose(0, 2, 1).astype(BF16)
    u_b, v_b = peer_u.astype(BF16), peer_v.astype(BF16)

    s5_ops = jax.vmap(_s5_operators)(lam_re, lam_im, log_dt, b_re, b_im, c_re, c_im, d_skip)

    cond = _cond(c, w_ada, b_ada).reshape(bsz, n_mod, d)
    xf = x.astype(F32).reshape(t, d)
    delta, gate_prev = None, None
    for l in range(depth):
        mod = cond + ada_layer[l].astype(F32)
        shift1, scale1, gate1, shift2, scale2, gate2 = (mod[:, i] for i in range(n_mod))

        xf, h = _adaln_norm(xf, delta, gate_prev, norm1_g[l], scale1, shift1, seq)
        z_ssm = _matmul_channel_blocks(h, w_in_b, l, ssm_w, BF16, tm=1024, tn=512, steps=S5_CHUNK)
        z_attn = _matmul(h, w_in_b, l, ssm_w, attn_w + 2 * kv_w, BF16, tm=1024, tn=512)
        g = _s5_core(z_ssm, s5_ops, l, bsz)
        y_ssm = _glu_norm(g, w_glu_b, l, gn_ssm[l], steps=S5_CHUNK)
        y_attn = _attention(z_attn, 0, attn_w, kv_w, q_gain[l], k_gain[l], sinks[l], gn_attn[l], bsz)
        mixed = _matmul_concat(y_ssm, y_attn, w_out_b, l, BF16, tm=1024, tn=512)

        xf, h2 = _adaln_norm(xf, mixed, gate1, norm2_g[l], scale2, shift2, seq)
        qt = _matmul_nt(wq_t_b, l, h2, F32, tn=512)
        s1t, s2t, stats = _peer_topk(qt, peer_k1[l], peer_k2[l])
        delta = _peer_dense(h2, u_b, v_b, l, s1t, s2t, stats)
        gate_prev = gate2
    out = _residual_add(xf, delta, gate_prev, seq)
    return out.reshape(bsz, seq, d).astype(x.dtype)
```
